```python
import math
import jax
import jax.numpy as jnp
from jax import lax
import numpy as np

D_MODEL = 1024
BATCH = 8
SEQ = 2048
DEPTH = 2

CTX_LEN = 256
GRID_W = 64
EPS = 1e-6
N_MOD = 6

D_RNN = 1024
RNN_BLOCKS = 8
RNN_BLOCK = D_RNN // RNN_BLOCKS
CONV_W = 4
LRU_C = 8.0

FOURIER_GROUPS = 8
FOURIER_GD = 128
D_FOURIER = FOURIER_GROUPS * FOURIER_GD

N_HEADS = 8
HEAD_DIM = 64
V_DIM = 2 * HEAD_DIM
D_ATTN = N_HEADS * V_DIM
ROPE_BASE = 10000.0
Q_BLOCK = 128

N_BRANCH = 3
D_MIX = D_RNN + D_FOURIER + D_ATTN
D_IN = 2 * D_RNN + D_FOURIER + 3 * D_ATTN
SPLITS = (D_RNN, 2 * D_RNN, 2 * D_RNN + D_FOURIER, 2 * D_RNN + D_FOURIER + D_ATTN, 2 * D_RNN + D_FOURIER + 2 * D_ATTN)
D_FF = ((8 * D_MODEL + 3 * 256 - 1) // (3 * 256)) * 256

kernel_name = 'hybrid_rglru_fourier_diffattn_dit'


def rmsnorm(x, g=None):
    xf = x.astype(jnp.float32)
    y = xf * lax.rsqrt(jnp.mean(xf * xf, axis=-1, keepdims=True) + EPS)
    if g is not None:
        y = y * g.astype(jnp.float32)
    return y.astype(x.dtype)


def adaln_params(cond, w, b):
    m = jax.nn.silu(cond) @ w + b
    return jnp.split(m[..., None, :], N_MOD, axis=-1)


def modulate(x, shift, scale):
    return rmsnorm(x) * (1 + scale) + shift


def dwconv_centred(x, w, b):
    n = x.shape[1]
    left = CONV_W // 2
    xp = jnp.pad(x, ((0, 0), (left, CONV_W - 1 - left), (0, 0)))
    out = xp[:, 0:n] * w[0]
    for k in range(1, CONV_W):
        out = out + xp[:, k:k + n] * w[k]
    return out + b


def block_diag_linear(x, w, b):
    xb = x.reshape(x.shape[:-1] + (RNN_BLOCKS, RNN_BLOCK))
    y = jnp.einsum('blgi,gij->blgj', xb, w.astype(jnp.float32))
    return y.reshape(x.shape) + b.astype(jnp.float32)


def rglru_coeffs(u, wr, br, wi, bi, lam):
    uf = u.astype(jnp.float32)
    r = jax.nn.sigmoid(block_diag_linear(uf, wr, br))
    i = jax.nn.sigmoid(block_diag_linear(uf, wi, bi))
    log_a = LRU_C * r * jax.nn.log_sigmoid(lam.astype(jnp.float32))
    a = jnp.exp(log_a)
    mult = jnp.sqrt(-jnp.expm1(2.0 * log_a))
    return a, mult * i * uf


def _lru_combine(e1, e2):
    a1, b1 = e1
    a2, b2 = e2
    return a1 * a2, a2 * b1 + b2


def linear_scan(a, b, h0, reverse):
    if h0 is not None:
        edge = -1 if reverse else 0
        b = b.at[:, edge].add(a[:, edge] * h0)
    _, h = lax.associative_scan(_lru_combine, (a, b), reverse=reverse, axis=1)
    return h


def rglru_mixer(xr_c, gr_c, xr_l, gr_l, conv_w, conv_b, wr, br, wi, bi, lam, need_ctx_out):
    uc = dwconv_centred(xr_c, conv_w, conv_b)
    ul = dwconv_centred(xr_l, conv_w, conv_b)
    lat_dirs, ctx_dirs = [], []
    for d, reverse in ((0, False), (1, True)):
        ac, bc = rglru_coeffs(uc, wr[d], br[d], wi[d], bi[d], lam[d])
        hc = linear_scan(ac, bc, None, reverse)
        state = hc[:, 0] if reverse else hc[:, -1]
        al, bl = rglru_coeffs(ul, wr[d], br[d], wi[d], bi[d], lam[d])
        lat_dirs.append(linear_scan(al, bl, state, reverse))
        ctx_dirs.append(hc)
    yl = ((lat_dirs[0] + lat_dirs[1]) * jax.nn.gelu(gr_l.astype(jnp.float32))).astype(xr_l.dtype)
    if not need_ctx_out:
        return yl, None
    yc = ((ctx_dirs[0] + ctx_dirs[1]) * jax.nn.gelu(gr_c.astype(jnp.float32))).astype(xr_c.dtype)
    return yl, yc


def fourier_mix(xf):
    b, n, _ = xf.shape
    xg = xf.astype(jnp.float32).reshape(b, n, FOURIER_GROUPS, FOURIER_GD)
    y = jnp.fft.fft2(xg, axes=(1, 3), norm='ortho').real
    return y.reshape(b, n, D_FOURIER).astype(xf.dtype)


def axial_rope(n_lat):
    rows_count = n_lat // GRID_W
    rows = jnp.repeat(jnp.arange(rows_count), GRID_W).astype(jnp.float32)
    cols = jnp.tile(jnp.arange(GRID_W), rows_count).astype(jnp.float32)
    n_freq = HEAD_DIM // 4
    inv = ROPE_BASE ** (-jnp.arange(n_freq, dtype=jnp.float32) / n_freq)
    ang = jnp.concatenate([rows[:, None] * inv, cols[:, None] * inv], axis=-1)
    return jnp.cos(ang), jnp.sin(ang)


def apply_rope(t, cos, sin):
    c = cos[None, :, None, None, :]
    s = sin[None, :, None, None, :]
    t1, t2 = jnp.split(t.astype(jnp.float32), 2, axis=-1)
    return jnp.concatenate([t1 * c - t2 * s, t1 * s + t2 * c], axis=-1).astype(t.dtype)


def split_qk_heads(t):
    return t.reshape(t.shape[:2] + (N_HEADS, 2, HEAD_DIM))


def split_v_heads(t):
    return t.reshape(t.shape[:2] + (N_HEADS, V_DIM))


def diff_attend(q, k, v, lam, lam_init):
    s = jnp.einsum('bqhcd,bkhcd->bhcqk', q.astype(jnp.float32), k.astype(jnp.float32)) * HEAD_DIM ** -0.5
    p = jax.nn.softmax(s, axis=-1)
    w = p[:, :, 0] - lam * p[:, :, 1]
    o = jnp.einsum('bhqk,bkhv->bqhv', w, v.astype(jnp.float32))
    o = o * lax.rsqrt(jnp.mean(o * o, axis=-1, keepdims=True) + EPS)
    return o * (1.0 - lam_init)


def diff_attention(q_c, k_c, v_c, q_l, k_l, v_l, lam_vec, lam_init, need_ctx_out):
    b, n_lat = q_l.shape[0], q_l.shape[1]
    cos, sin = axial_rope(n_lat)
    ql = apply_rope(split_qk_heads(q_l), cos, sin)
    kl = apply_rope(split_qk_heads(k_l), cos, sin)
    kc = split_qk_heads(k_c)
    vc = split_v_heads(v_c)
    k_all = jnp.concatenate([kl, kc], axis=1)
    v_all = jnp.concatenate([split_v_heads(v_l), vc], axis=1)
    lv = lam_vec.astype(jnp.float32)
    lam = jnp.exp(jnp.sum(lv[0] * lv[1])) - jnp.exp(jnp.sum(lv[2] * lv[3])) + lam_init
    n_blk = n_lat // Q_BLOCK
    qb = ql.reshape(b, n_blk, Q_BLOCK, N_HEADS, 2, HEAD_DIM).swapaxes(0, 1)
    ob = lax.map(lambda qi: diff_attend(qi, k_all, v_all, lam, lam_init), qb)
    yl = ob.swapaxes(0, 1).reshape(b, n_lat, D_ATTN).astype(q_l.dtype)
    if not need_ctx_out:
        return yl, None
    yc = diff_attend(split_qk_heads(q_c), kc, vc, lam, lam_init)
    return yl, yc.reshape(b, q_c.shape[1], D_ATTN).astype(q_c.dtype)


def merge_branches(h, ya, yb, yc, w_branch, w_gate, b_gate, w_out):
    pa = ya @ w_branch[:D_RNN]
    pb = yb @ w_branch[D_RNN:D_RNN + D_FOURIER]
    pc = yc @ w_branch[D_RNN + D_FOURIER:]
    g = jax.nn.sigmoid((h @ w_gate + b_gate).astype(jnp.float32)).astype(h.dtype)
    ga, gb, gc = jnp.split(g, N_BRANCH, axis=-1)
    return (ga * pa + gb * pb + gc * pc) @ w_out


def swiglu(h, w1, w3, w2):
    return (jax.nn.silu(h @ w1) * (h @ w3)) @ w2


def setup_inputs(seed: int = 0) -> dict:
    key = jax.random.key(seed)
    ks = jax.random.split(key, 24)
    f32 = jnp.float32

    def nrm(k, shape, fan_in, gain=1.0):
        return gain * fan_in ** -0.5 * jax.random.normal(k, shape, f32)

    def small(k, shape):
        return 0.02 * jax.random.normal(k, shape, f32)

    u = jax.random.uniform(ks[13], (DEPTH, 2, D_RNN), f32, 0.9, 0.999)
    a_base = u ** (1.0 / LRU_C)
    return {
        'x': jax.random.normal(ks[0], (BATCH, SEQ, D_MODEL), f32),
        'c': jax.random.normal(ks[1], (BATCH, D_MODEL), f32),
        'ctx': jax.random.normal(ks[2], (BATCH, CTX_LEN, D_MODEL), f32),
        'c_ctx': jax.random.normal(ks[3], (D_MODEL,), f32),
        'ada_w': nrm(ks[4], (DEPTH, D_MODEL, N_MOD * D_MODEL), D_MODEL, 0.5),
        'ada_b': small(ks[5], (DEPTH, N_MOD * D_MODEL)),
        'w_in': nrm(ks[6], (DEPTH, D_MODEL, D_IN), D_MODEL),
        'rnn_conv_w': nrm(ks[7], (DEPTH, CONV_W, D_RNN), CONV_W),
        'rnn_conv_b': small(ks[8], (DEPTH, D_RNN)),
        'rnn_wr': nrm(ks[9], (DEPTH, 2, RNN_BLOCKS, RNN_BLOCK, RNN_BLOCK), RNN_BLOCK),
        'rnn_br': small(ks[10], (DEPTH, 2, D_RNN)),
        'rnn_wi': nrm(ks[11], (DEPTH, 2, RNN_BLOCKS, RNN_BLOCK, RNN_BLOCK), RNN_BLOCK),
        'rnn_bi': small(ks[12], (DEPTH, 2, D_RNN)),
        'rnn_lambda': jnp.log(a_base) - jnp.log1p(-a_base),
        'attn_lambda': 0.1 * jax.random.normal(ks[14], (DEPTH, 4, HEAD_DIM), f32),
        'w_branch': nrm(ks[15], (DEPTH, D_MIX, D_MODEL), D_RNN),
        'w_gate': nrm(ks[16], (DEPTH, D_MODEL, N_BRANCH * D_MODEL), D_MODEL),
        'b_gate': small(ks[17], (DEPTH, N_BRANCH * D_MODEL)),
        'w_out': nrm(ks[18], (DEPTH, D_MODEL, D_MODEL), D_MODEL),
        'ffn_w1': nrm(ks[19], (DEPTH, D_MODEL, D_FF), D_MODEL),
        'ffn_w3': nrm(ks[20], (DEPTH, D_MODEL, D_FF), D_MODEL),
        'ffn_w2': nrm(ks[21], (DEPTH, D_FF, D_MODEL), D_FF),
        'final_g': 1.0 + small(ks[22], (D_MODEL,)),
    }


def reference(x, c, ctx, c_ctx, ada_w, ada_b, w_in, rnn_conv_w, rnn_conv_b, rnn_wr, rnn_br, rnn_wi, rnn_bi, rnn_lambda, attn_lambda, w_branch, w_gate, b_gate, w_out, ffn_w1, ffn_w3, ffn_w2, final_g):
    xl, xc = x, ctx
    for l in range(DEPTH):
        last = l == DEPTH - 1
        lam_init = 0.8 - 0.6 * math.exp(-0.3 * l)
        sh1_l, sc1_l, g1_l, sh2_l, sc2_l, g2_l = adaln_params(c, ada_w[l], ada_b[l])
        sh1_c, sc1_c, g1_c, sh2_c, sc2_c, g2_c = adaln_params(c_ctx, ada_w[l], ada_b[l])

        hl = modulate(xl, sh1_l, sc1_l)
        hc = modulate(xc, sh1_c, sc1_c)
        xr_l, gr_l, xf_l, q_l, k_l, v_l = jnp.split(hl @ w_in[l], SPLITS, axis=-1)
        if last:
            xr_c = hc @ w_in[l][:, :SPLITS[0]]
            k_c, v_c = jnp.split(hc @ w_in[l][:, SPLITS[3]:], 2, axis=-1)
            gr_c = xf_c = q_c = None
        else:
            xr_c, gr_c, xf_c, q_c, k_c, v_c = jnp.split(hc @ w_in[l], SPLITS, axis=-1)

        ya_l, ya_c = rglru_mixer(xr_c, gr_c, xr_l, gr_l, rnn_conv_w[l], rnn_conv_b[l], rnn_wr[l], rnn_br[l], rnn_wi[l], rnn_bi[l], rnn_lambda[l], not last)
        yc_l, yc_c = diff_attention(q_c, k_c, v_c, q_l, k_l, v_l, attn_lambda[l], lam_init, not last)
        yb_l = fourier_mix(xf_l)
        xl = xl + g1_l * merge_branches(hl, ya_l, yb_l, yc_l, w_branch[l], w_gate[l], b_gate[l], w_out[l])

        xl = xl + g2_l * swiglu(modulate(xl, sh2_l, sc2_l), ffn_w1[l], ffn_w3[l], ffn_w2[l])

        if not last:
            yb_c = fourier_mix(xf_c)
            xc = xc + g1_c * merge_branches(hc, ya_c, yb_c, yc_c, w_branch[l], w_gate[l], b_gate[l], w_out[l])
            xc = xc + g2_c * swiglu(modulate(xc, sh2_c, sc2_c), ffn_w1[l], ffn_w3[l], ffn_w2[l])

    return rmsnorm(xl, final_g)
```

```python
import functools
import math

import jax
import jax.numpy as jnp
from jax import lax
from jax.experimental import pallas as pl
from jax.experimental.pallas import tpu as pltpu

F32 = jnp.float32
BF16 = jnp.bfloat16

EPS = 1e-6
N_MOD = 6
GRID_W = 64
RNN_BLOCK = 128
CONV_W = 4
LRU_C = 8.0
FOURIER_GD = 128
N_HEADS = 8
HEAD_DIM = 64
V_DIM = 2 * HEAD_DIM
ROPE_BASE = 10000.0

LANES = 128
SUBLANES = 8
TM = 256
VMEM_LIMIT = 56 * 1024 * 1024


def _params(n_axes):
    return pltpu.CompilerParams(
        dimension_semantics=("parallel",) * n_axes, vmem_limit_bytes=VMEM_LIMIT)


def _const_spec(shape):
    nd = len(shape)
    return pl.BlockSpec(shape, lambda *_: (0,) * nd, pipeline_mode=pl.Buffered(1))


def _rms(x):
    return x * lax.rsqrt(jnp.mean(x * x, axis=-1, keepdims=True) + EPS)


def _adaln_kernel(c_ref, w_ref, b_ref, o_ref):
    c = c_ref[...]
    s = (c * jax.nn.sigmoid(c)).astype(BF16)
    o_ref[0] = jnp.dot(s, w_ref[0].astype(BF16), preferred_element_type=F32) + b_ref[0]


def _adaln(cond, ada_w, ada_b):
    depth, d, n = ada_w.shape
    r = cond.shape[0]
    tn = 1536
    return pl.pallas_call(
        _adaln_kernel,
        grid=(depth, n // tn),
        in_specs=[
            pl.BlockSpec((r, d), lambda l, j: (0, 0)),
            pl.BlockSpec((1, d, tn), lambda l, j: (l, 0, j)),
            pl.BlockSpec((1, 1, tn), lambda l, j: (l, 0, j)),
        ],
        out_specs=pl.BlockSpec((1, r, tn), lambda l, j: (l, 0, j)),
        out_shape=jax.ShapeDtypeStruct((depth, r, n), F32),
        compiler_params=_params(2),
        name="adaln",
    )(cond, ada_w, ada_b)


def _inproj_kernel(x_ref, mod_ref, w_ref, wf_ref, cos_ref, sin_ref,
                   h_ref, xr_ref, gr_ref, xc_ref, xs_ref, q_ref, k_ref, v_ref):
    d = x_ref.shape[-1]
    x = x_ref[0]
    h = _rms(x) * (1.0 + mod_ref[0, 1:2, :]) + mod_ref[0, 0:1, :]
    hb = h.astype(BF16)
    h_ref[0] = hb

    def proj(j):
        return jnp.dot(hb, w_ref[:, j * d:(j + 1) * d], preferred_element_type=F32)

    xr_ref[0] = proj(0)
    gr_ref[0] = proj(1)

    xf = proj(2).astype(BF16)
    for g in range(d // FOURIER_GD):
        sl = slice(g * FOURIER_GD, (g + 1) * FOURIER_GD)
        z = jnp.dot(xf[:, sl], wf_ref[...], preferred_element_type=F32)
        xc_ref[0, :, sl] = z[:, :FOURIER_GD].astype(BF16)
        xs_ref[0, :, sl] = z[:, FOURIER_GD:].astype(BF16)

    cos = cos_ref[...]
    sin = sin_ref[...]
    lane = lax.broadcasted_iota(jnp.int32, cos.shape, 1)
    first_half = (lane % HEAD_DIM) < (HEAD_DIM // 2)

    def rope(t, o_ref, scale):
        for hh in range(d // LANES):
            sl = slice(hh * LANES, (hh + 1) * LANES)
            th = t[:, sl]
            swapped = jnp.where(first_half,
                                pltpu.roll(th, LANES - HEAD_DIM // 2, 1),
                                pltpu.roll(th, HEAD_DIM // 2, 1))
            o_ref[0, :, sl] = ((th * cos + swapped * sin) * scale).astype(BF16)

    rope(proj(3), q_ref, HEAD_DIM ** -0.5)
    rope(proj(4), k_ref, 1.0)
    v_ref[0] = proj(5).astype(BF16)


def _inproj(x_all, mod, w_in, wf, cos_t, sin_t):
    b, t, d = x_all.shape
    nt = t // TM
    tok = pl.BlockSpec((1, TM, d), lambda bb, i: (bb, i, 0))
    f32_out = jax.ShapeDtypeStruct((b, t, d), F32)
    bf_out = jax.ShapeDtypeStruct((b, t, d), BF16)
    return pl.pallas_call(
        _inproj_kernel,
        grid=(b, nt),
        in_specs=[
            tok,
            pl.BlockSpec((1, N_MOD, d), lambda bb, i: (jnp.where(i == 0, b, bb), 0, 0)),
            _const_spec(w_in.shape),
            _const_spec(wf.shape),
            pl.BlockSpec((TM, LANES), lambda bb, i: (i, 0)),
            pl.BlockSpec((TM, LANES), lambda bb, i: (i, 0)),
        ],
        out_specs=[tok] * 8,
        out_shape=[bf_out, f32_out, f32_out, bf_out, bf_out, bf_out, bf_out, bf_out],
        compiler_params=_params(2),
        name="inproj",
    )(x_all, mod, w_in, wf, cos_t, sin_t)


def _log_sigmoid(x):
    return jnp.minimum(x, 0.0) - jnp.log(1.0 + jnp.exp(-jnp.abs(x)))


def _rglru_kernel(xr_ref, gr_ref, cw_ref, cb_ref, wr_ref, br_ref, wi_ref, bi_ref, lam_ref,
                  ya_ref, a0, b0, a1, b1, h0, p0, h1, p1, *, n_ctx):
    t_all, c = xr_ref.shape[1], xr_ref.shape[2]
    n_lat = t_all - n_ctx
    x = xr_ref[0]
    t = lax.broadcasted_iota(jnp.int32, (t_all, c), 0)
    is_ctx = t < n_ctx
    pos = jnp.where(is_ctx, t, t - n_ctx)
    seg_len = jnp.where(is_ctx, n_ctx, n_lat)
    xm2 = jnp.where(pos >= 2, pltpu.roll(x, 2, 0), 0.0)
    xm1 = jnp.where(pos >= 1, pltpu.roll(x, 1, 0), 0.0)
    xp1 = jnp.where(pos < seg_len - 1, pltpu.roll(x, t_all - 1, 0), 0.0)
    u = xm2 * cw_ref[0:1, :]
    u = u + xm1 * cw_ref[1:2, :]
    u = u + x * cw_ref[2:3, :]
    u = u + xp1 * cw_ref[3:4, :]
    u = u + cb_ref[...]
    ub = u.astype(BF16)

    for dd, (a_s, b_s) in enumerate(((a0, b0), (a1, b1))):
        r = jax.nn.sigmoid(jnp.dot(ub, wr_ref[dd, 0], preferred_element_type=F32) + br_ref[dd:dd + 1, :])
        gate_i = jax.nn.sigmoid(jnp.dot(ub, wi_ref[dd, 0], preferred_element_type=F32) + bi_ref[dd:dd + 1, :])
        log_a = LRU_C * r * _log_sigmoid(lam_ref[dd:dd + 1, :])
        a_s[...] = jnp.exp(log_a)
        b_s[...] = jnp.sqrt(1.0 - jnp.exp(2.0 * log_a)) * gate_i * u

    def scan_segment(base, length, init_f, init_r):
        step = length // SUBLANES

        def body(j, carry):
            hf, pf, hr, pr = carry
            rows_f = pl.ds(base + j, SUBLANES, stride=step)
            af = a0[rows_f, :]
            hf = af * hf + b0[rows_f, :]
            pf = af * pf
            h0[rows_f, :] = hf
            p0[rows_f, :] = pf
            rows_r = pl.ds(base + step - 1 - j, SUBLANES, stride=step)
            ar = a1[rows_r, :]
            hr = ar * hr + b1[rows_r, :]
            pr = ar * pr
            h1[rows_r, :] = hr
            p1[rows_r, :] = pr
            return hf, pf, hr, pr

        zeros = jnp.zeros((SUBLANES, c), F32)
        ones = jnp.ones((SUBLANES, c), F32)
        lax.fori_loop(0, step, body, (zeros, ones, zeros, ones))

        carry_f = init_f
        for s in range(SUBLANES):
            rows = slice(base + s * step, base + (s + 1) * step)
            seg = h0[rows, :] + p0[rows, :] * carry_f
            h0[rows, :] = seg
            carry_f = seg[step - 1:step, :]
        carry_r = init_r
        for s in reversed(range(SUBLANES)):
            rows = slice(base + s * step, base + (s + 1) * step)
            seg = h1[rows, :] + p1[rows, :] * carry_r
            h1[rows, :] = seg
            carry_r = seg[0:1, :]
        return carry_f, carry_r

    zero_state = jnp.zeros((1, c), F32)
    state_f, state_r = scan_segment(0, n_ctx, zero_state, zero_state)
    scan_segment(n_ctx, n_lat, state_f, state_r)
    ya_ref[0] = ((h0[...] + h1[...]) * jax.nn.gelu(gr_ref[0])).astype(BF16)


def _rglru(xr, gr, conv_w, conv_b, wr, br, wi, bi, lam, n_ctx):
    b, t, d = xr.shape
    c = RNN_BLOCK
    seq = pl.BlockSpec((1, t, c), lambda bb, g: (bb, 0, g))
    vec = lambda rows: pl.BlockSpec((rows, c), lambda bb, g: (0, g))
    mat = pl.BlockSpec((2, 1, c, c), lambda bb, g: (0, g, 0, 0))
    return pl.pallas_call(
        functools.partial(_rglru_kernel, n_ctx=n_ctx),
        grid=(b, d // c),
        in_specs=[seq, seq, vec(CONV_W), vec(1), mat, vec(2), mat, vec(2), vec(2)],
        out_specs=seq,
        out_shape=jax.ShapeDtypeStruct((b, t, d), BF16),
        scratch_shapes=[pltpu.VMEM((t, c), F32)] * 8,
        compiler_params=_params(2),
        name="rglru",
    )(xr, gr, conv_w, conv_b, wr, br, wi, bi, lam)


def _fourier_kernel(al_ref, ac_ref, xc_ref, xs_ref, y_ref, *, n_ctx, scale_ctx, scale_lat):
    i = pl.program_id(1)
    n_lat = xc_ref.shape[1] - n_ctx

    def mix(a_ref, lo, n, scale):
        y = jnp.dot(a_ref[:, :n], xc_ref[0, lo:lo + n, :], preferred_element_type=F32)
        y = y + jnp.dot(a_ref[:, n:], xs_ref[0, lo:lo + n, :], preferred_element_type=F32)
        y_ref[0] = (y * scale).astype(BF16)

    @pl.when(i == 0)
    def _():
        mix(ac_ref, 0, n_ctx, scale_ctx)

    @pl.when(i > 0)
    def _():
        mix(al_ref, n_ctx, n_lat, scale_lat)


def _fourier(a_lat, a_ctx, xc, xs, n_ctx):
    b, t, d = xc.shape
    n_lat = t - n_ctx
    assert n_ctx == TM
    slab = pl.BlockSpec((1, t, d), lambda bb, i: (bb, 0, 0))
    return pl.pallas_call(
        functools.partial(_fourier_kernel, n_ctx=n_ctx,
                          scale_ctx=(n_ctx * FOURIER_GD) ** -0.5,
                          scale_lat=(n_lat * FOURIER_GD) ** -0.5),
        grid=(b, t // TM),
        in_specs=[
            pl.BlockSpec((TM, 2 * n_lat), lambda bb, i: (jnp.maximum(i - 1, 0), 0)),
            _const_spec(a_ctx.shape),
            slab, slab,
        ],
        out_specs=pl.BlockSpec((1, TM, d), lambda bb, i: (bb, i, 0)),
        out_shape=jax.ShapeDtypeStruct((b, t, d), BF16),
        compiler_params=_params(2),
        name="fourier",
    )(a_lat, a_ctx, xc, xs)


def _attn_kernel(lam_ref, q_ref, k_ref, v_ref, o_ref, *, n_ctx, lam_init):
    i = pl.program_id(2)
    lv = lam_ref[...]
    lam = (jnp.exp(jnp.sum(lv[0:1] * lv[1:2], axis=1, keepdims=True))
           - jnp.exp(jnp.sum(lv[2:3] * lv[3:4], axis=1, keepdims=True)) + lam_init)
    q = q_ref[0]
    lane = lax.broadcasted_iota(jnp.int32, q.shape, 1)
    zero = jnp.zeros_like(q)
    q_halves = (jnp.where(lane < HEAD_DIM, q, zero), jnp.where(lane >= HEAD_DIM, q, zero))

    def softmax(qh, k):
        s = lax.dot_general(qh, k, (((1,), (1,)), ((), ())), preferred_element_type=F32)
        p = jnp.exp(s - jnp.max(s, axis=-1, keepdims=True))
        return p * (1.0 / jnp.sum(p, axis=-1, keepdims=True))

    def attend(k, v):
        w = softmax(q_halves[0], k) - lam * softmax(q_halves[1], k)
        o = jnp.dot(w.astype(BF16), v, preferred_element_type=F32)
        o_ref[0] = (_rms(o) * (1.0 - lam_init)).astype(BF16)

    @pl.when(i == 0)
    def _():
        attend(k_ref[0, :n_ctx, :], v_ref[0, :n_ctx, :])

    @pl.when(i > 0)
    def _():
        attend(k_ref[0], v_ref[0])


def _attention(lam_vec, q, k, v, n_ctx, lam_init):
    b, t, d = q.shape
    assert n_ctx == TM
    kv = pl.BlockSpec((1, t, V_DIM), lambda bb, hh, i: (bb, 0, hh))
    qo = pl.BlockSpec((1, TM, V_DIM), lambda bb, hh, i: (bb, i, hh))
    return pl.pallas_call(
        functools.partial(_attn_kernel, n_ctx=n_ctx, lam_init=lam_init),
        grid=(b, d // V_DIM, t // TM),
        in_specs=[pl.BlockSpec(lam_vec.shape, lambda bb, hh, i: (0, 0)), qo, kv, kv],
        out_specs=qo,
        out_shape=jax.ShapeDtypeStruct((b, t, d), BF16),
        compiler_params=_params(3),
        name="diffattn",
    )(lam_vec, q, k, v)


def _merge_kernel(x_ref, mod_ref, h_ref, ya_ref, yb_ref, yc_ref, wb_ref, wg_ref, bg_ref, wo_ref, o_ref):
    d = x_ref.shape[-1]
    g = jax.nn.sigmoid(jnp.dot(h_ref[0], wg_ref[...], preferred_element_type=F32) + bg_ref[...])
    m = g[:, :d] * jnp.dot(ya_ref[0], wb_ref[:d, :], preferred_element_type=F32)
    m = m + g[:, d:2 * d] * jnp.dot(yb_ref[0], wb_ref[d:2 * d, :], preferred_element_type=F32)
    m = m + g[:, 2 * d:] * jnp.dot(yc_ref[0], wb_ref[2 * d:, :], preferred_element_type=F32)
    out = jnp.dot(m.astype(BF16), wo_ref[...], preferred_element_type=F32)
    o_ref[0] = x_ref[0] + mod_ref[0, 2:3, :] * out


def _merge(x_all, mod, h, ya, yb, yc, w_branch, w_gate, b_gate, w_out, tile0):
    b, t, d = x_all.shape
    nt = t // TM - tile0
    tok = pl.BlockSpec((1, TM, d), lambda bb, i: (bb, i + tile0, 0))
    return pl.pallas_call(
        _merge_kernel,
        grid=(b, nt),
        in_specs=[
            tok,
            pl.BlockSpec((1, N_MOD, d), lambda bb, i: (jnp.where(i + tile0 == 0, b, bb), 0, 0)),
            tok, tok, tok, tok,
            _const_spec(w_branch.shape), _const_spec(w_gate.shape), _const_spec(b_gate.shape),
            _const_spec(w_out.shape),
        ],
        out_specs=pl.BlockSpec((1, TM, d), lambda bb, i: (bb, i, 0)),
        out_shape=jax.ShapeDtypeStruct((b, nt * TM, d), F32),
        compiler_params=_params(2),
        name="merge",
    )(x_all, mod, h, ya, yb, yc, w_branch, w_gate, b_gate, w_out)


def _ffn_kernel(x_ref, mod_ref, w1_ref, w3_ref, w2_ref, fg_ref, o_ref, *, final_norm):
    x = x_ref[0]
    hb = (_rms(x) * (1.0 + mod_ref[0, 4:5, :]) + mod_ref[0, 3:4, :]).astype(BF16)
    a = jnp.dot(hb, w1_ref[...], preferred_element_type=F32)
    g = jnp.dot(hb, w3_ref[...], preferred_element_type=F32)
    s = (a * jax.nn.sigmoid(a) * g).astype(BF16)
    y = x + mod_ref[0, 5:6, :] * jnp.dot(s, w2_ref[...], preferred_element_type=F32)
    if final_norm:
        y = _rms(y) * fg_ref[...]
    o_ref[0] = y


def _ffn(x_all, mod, w1, w3, w2, final_g, ctx_tiles, final_norm):
    b, t, d = x_all.shape
    tok = pl.BlockSpec((1, TM, d), lambda bb, i: (bb, i, 0))
    return pl.pallas_call(
        functools.partial(_ffn_kernel, final_norm=final_norm),
        grid=(b, t // TM),
        in_specs=[
            tok,
            pl.BlockSpec((1, N_MOD, d), lambda bb, i: (jnp.where(i < ctx_tiles, b, bb), 0, 0)),
            _const_spec(w1.shape), _const_spec(w3.shape), _const_spec(w2.shape),
            _const_spec(final_g.shape),
        ],
        out_specs=tok,
        out_shape=jax.ShapeDtypeStruct((b, t, d), F32),
        compiler_params=_params(2),
        name="ffn",
    )(x_all, mod, w1, w3, w2, final_g)


def _rope_tables(n_ctx, n_lat):
    p = jnp.arange(n_lat)
    rows = (p // GRID_W).astype(F32)
    cols = (p % GRID_W).astype(F32)
    n_freq = HEAD_DIM // 4
    inv = ROPE_BASE ** (-jnp.arange(n_freq, dtype=F32) / n_freq)
    ang = jnp.concatenate([rows[:, None] * inv, cols[:, None] * inv], axis=-1)
    cos, sin = jnp.cos(ang), jnp.sin(ang)
    reps = LANES // HEAD_DIM
    cos_l = jnp.tile(jnp.concatenate([cos, cos], axis=-1), (1, reps))
    sin_l = jnp.tile(jnp.concatenate([-sin, sin], axis=-1), (1, reps))
    cos_t = jnp.concatenate([jnp.ones((n_ctx, LANES), F32), cos_l], axis=0)
    sin_t = jnp.concatenate([jnp.zeros((n_ctx, LANES), F32), sin_l], axis=0)
    return cos_t, sin_t


def _dft_cos_sin(n):
    k = jnp.arange(n, dtype=jnp.int32)
    ang = ((k[:, None] * k[None, :]) % n).astype(F32) * (2.0 * math.pi / n)
    return jnp.cos(ang), jnp.sin(ang)


def _dft_position_matrix(n):
    c, s = _dft_cos_sin(n)
    return jnp.concatenate([c, -s], axis=1).astype(BF16)


def kernel(x, c, ctx, c_ctx, ada_w, ada_b, w_in, rnn_conv_w, rnn_conv_b, rnn_wr, rnn_br, rnn_wi, rnn_bi,
           rnn_lambda, attn_lambda, w_branch, w_gate, b_gate, w_out, ffn_w1, ffn_w3, ffn_w2, final_g):
    b, n_lat, d = x.shape
    n_ctx = ctx.shape[1]
    depth = ada_w.shape[0]
    ctx_tiles = n_ctx // TM

    cond = jnp.zeros((2 * SUBLANES, d), F32).at[:b].set(c).at[b].set(c_ctx)
    mod_all = _adaln(cond, ada_w, ada_b.reshape(depth, 1, N_MOD * d))

    cos_t, sin_t = _rope_tables(n_ctx, n_lat)
    cc, sc = _dft_cos_sin(FOURIER_GD)
    wf = jnp.concatenate([cc, sc], axis=1).astype(BF16)
    a_lat = _dft_position_matrix(n_lat)
    a_ctx = _dft_position_matrix(n_ctx)
    fg = final_g.reshape(1, d)

    x_all = jnp.concatenate([ctx, x], axis=1)
    for l in range(depth):
        last = l == depth - 1
        lam_init = 0.8 - 0.6 * math.exp(-0.3 * l)
        mod = mod_all[l].reshape(2 * SUBLANES, N_MOD, d)
        h, xr, gr, xc, xs, q, k, v = _inproj(x_all, mod, w_in[l].astype(BF16), wf, cos_t, sin_t)
        ya = _rglru(xr, gr, rnn_conv_w[l], rnn_conv_b[l].reshape(1, d), rnn_wr[l].astype(BF16), rnn_br[l],
                    rnn_wi[l].astype(BF16), rnn_bi[l], rnn_lambda[l], n_ctx)
        yb = _fourier(a_lat, a_ctx, xc, xs, n_ctx)
        yc = _attention(attn_lambda[l], q, k, v, n_ctx, lam_init)
        tile0 = ctx_tiles if last else 0
        x_mid = _merge(x_all, mod, h, ya, yb, yc, w_branch[l].astype(BF16), w_gate[l].astype(BF16),
                       b_gate[l].reshape(1, -1), w_out[l].astype(BF16), tile0)
        x_all = _ffn(x_mid, mod, ffn_w1[l].astype(BF16), ffn_w3[l].astype(BF16), ffn_w2[l].astype(BF16),
                     fg, ctx_tiles - tile0, last)
    return x_all
```

```python
import functools
import math

import jax
import jax.numpy as jnp
from jax import lax
from jax.experimental import pallas as pl
from jax.experimental.pallas import tpu as pltpu

F32 = jnp.float32
BF16 = jnp.bfloat16

EPS = 1e-6
N_MOD = 6
GRID_W = 64
RNN_BLOCK = 128
CONV_W = 4
LRU_C = 8.0
FOURIER_GD = 128
N_HEADS = 8
HEAD_DIM = 64
V_DIM = 2 * HEAD_DIM
ROPE_BASE = 10000.0

LANES = 128
SUBLANES = 8
TM = 256
KEY_CHUNK = 256
V_EXT = 2 * V_DIM
LOG2_E = math.log2(math.e)
VMEM_LIMIT = 56 * 1024 * 1024


def _params(n_axes):
    return pltpu.CompilerParams(
        dimension_semantics=("parallel",) * n_axes, vmem_limit_bytes=VMEM_LIMIT)


def _const_spec(shape):
    nd = len(shape)
    return pl.BlockSpec(shape, lambda *_: (0,) * nd, pipeline_mode=pl.Buffered(1))


def _rms(x):
    return x * lax.rsqrt(jnp.mean(x * x, axis=-1, keepdims=True) + EPS)


def _adaln_kernel(c_ref, w_ref, b_ref, o_ref):
    c = c_ref[...]
    s = (c * jax.nn.sigmoid(c)).astype(BF16)
    o_ref[0] = jnp.dot(s, w_ref[0].astype(BF16), preferred_element_type=F32) + b_ref[0]


def _adaln(cond, ada_w, ada_b):
    depth, d, n = ada_w.shape
    r = cond.shape[0]
    tn = 1536
    return pl.pallas_call(
        _adaln_kernel,
        grid=(depth, n // tn),
        in_specs=[
            pl.BlockSpec((r, d), lambda l, j: (0, 0)),
            pl.BlockSpec((1, d, tn), lambda l, j: (l, 0, j)),
            pl.BlockSpec((1, 1, tn), lambda l, j: (l, 0, j)),
        ],
        out_specs=pl.BlockSpec((1, r, tn), lambda l, j: (l, 0, j)),
        out_shape=jax.ShapeDtypeStruct((depth, r, n), F32),
        compiler_params=_params(2),
        name="adaln",
    )(cond, ada_w, ada_b)


def _inproj_kernel(x_ref, mod_ref, w_ref, wf_ref, cos_ref, sin_ref,
                   h_ref, xr_ref, gr_ref, xc_ref, xs_ref, q_ref, k_ref, v_ref):
    d = x_ref.shape[-1]
    x = x_ref[0]
    h = _rms(x) * (1.0 + mod_ref[0, 1:2, :]) + mod_ref[0, 0:1, :]
    hb = h.astype(BF16)
    h_ref[0] = hb

    def proj(j):
        return jnp.dot(hb, w_ref[:, j * d:(j + 1) * d], preferred_element_type=F32)

    xr_ref[0] = proj(0)
    gr_ref[0] = proj(1)

    xf = proj(2).astype(BF16)
    for g in range(d // FOURIER_GD):
        sl = slice(g * FOURIER_GD, (g + 1) * FOURIER_GD)
        z = jnp.dot(xf[:, sl], wf_ref[...], preferred_element_type=F32)
        xc_ref[0, :, sl] = z[:, :FOURIER_GD].astype(BF16)
        xs_ref[0, :, sl] = z[:, FOURIER_GD:].astype(BF16)

    cos = cos_ref[...]
    sin = sin_ref[...]
    lane = lax.broadcasted_iota(jnp.int32, cos.shape, 1)
    first_half = (lane % HEAD_DIM) < (HEAD_DIM // 2)

    def rope(t, o_ref, scale):
        for hh in range(d // LANES):
            sl = slice(hh * LANES, (hh + 1) * LANES)
            th = t[:, sl]
            swapped = jnp.where(first_half,
                                pltpu.roll(th, LANES - HEAD_DIM // 2, 1),
                                pltpu.roll(th, HEAD_DIM // 2, 1))
            o_ref[0, :, sl] = ((th * cos + swapped * sin) * scale).astype(BF16)

    rope(proj(3), q_ref, HEAD_DIM ** -0.5 * LOG2_E)
    rope(proj(4), k_ref, 1.0)

    v = proj(5).astype(BF16)
    ones = jnp.ones((v.shape[0], V_EXT - V_DIM), BF16)
    for hh in range(d // V_DIM):
        v_ref[0, :, hh * V_EXT:hh * V_EXT + V_DIM] = v[:, hh * V_DIM:(hh + 1) * V_DIM]
        v_ref[0, :, hh * V_EXT + V_DIM:(hh + 1) * V_EXT] = ones


def _inproj(x_all, mod, w_in, wf, cos_t, sin_t):
    b, t, d = x_all.shape
    nt = t // TM
    d_ext = d // V_DIM * V_EXT
    tok = pl.BlockSpec((1, TM, d), lambda bb, i: (bb, i, 0))
    f32_out = jax.ShapeDtypeStruct((b, t, d), F32)
    bf_out = jax.ShapeDtypeStruct((b, t, d), BF16)
    return pl.pallas_call(
        _inproj_kernel,
        grid=(b, nt),
        in_specs=[
            tok,
            pl.BlockSpec((1, N_MOD, d), lambda bb, i: (jnp.where(i == 0, b, bb), 0, 0)),
            _const_spec(w_in.shape),
            _const_spec(wf.shape),
            pl.BlockSpec((TM, LANES), lambda bb, i: (i, 0)),
            pl.BlockSpec((TM, LANES), lambda bb, i: (i, 0)),
        ],
        out_specs=[tok] * 7 + [pl.BlockSpec((1, TM, d_ext), lambda bb, i: (bb, i, 0))],
        out_shape=[bf_out, f32_out, f32_out, bf_out, bf_out, bf_out, bf_out,
                   jax.ShapeDtypeStruct((b, t, d_ext), BF16)],
        compiler_params=_params(2),
        name="inproj",
    )(x_all, mod, w_in, wf, cos_t, sin_t)


def _log_sigmoid(x):
    return jnp.minimum(x, 0.0) - jnp.log(1.0 + jnp.exp(-jnp.abs(x)))


def _rglru_kernel(xr_ref, gr_ref, cw_ref, cb_ref, wr_ref, br_ref, wi_ref, bi_ref, lam_ref,
                  ya_ref, a0, b0, a1, b1, h0, p0, h1, p1, *, n_ctx):
    t_all, c = xr_ref.shape[1], xr_ref.shape[2]
    n_lat = t_all - n_ctx
    x = xr_ref[0]
    t = lax.broadcasted_iota(jnp.int32, (t_all, c), 0)
    is_ctx = t < n_ctx
    pos = jnp.where(is_ctx, t, t - n_ctx)
    seg_len = jnp.where(is_ctx, n_ctx, n_lat)
    xm2 = jnp.where(pos >= 2, pltpu.roll(x, 2, 0), 0.0)
    xm1 = jnp.where(pos >= 1, pltpu.roll(x, 1, 0), 0.0)
    xp1 = jnp.where(pos < seg_len - 1, pltpu.roll(x, t_all - 1, 0), 0.0)
    u = xm2 * cw_ref[0:1, :]
    u = u + xm1 * cw_ref[1:2, :]
    u = u + x * cw_ref[2:3, :]
    u = u + xp1 * cw_ref[3:4, :]
    u = u + cb_ref[...]
    ub = u.astype(BF16)

    for dd, (a_s, b_s) in enumerate(((a0, b0), (a1, b1))):
        r = jax.nn.sigmoid(jnp.dot(ub, wr_ref[dd, 0], preferred_element_type=F32) + br_ref[dd:dd + 1, :])
        gate_i = jax.nn.sigmoid(jnp.dot(ub, wi_ref[dd, 0], preferred_element_type=F32) + bi_ref[dd:dd + 1, :])
        log_a = LRU_C * r * _log_sigmoid(lam_ref[dd:dd + 1, :])
        a_s[...] = jnp.exp(log_a)
        b_s[...] = jnp.sqrt(1.0 - jnp.exp(2.0 * log_a)) * gate_i * u

    def scan_segment(base, length, init_f, init_r):
        step = length // SUBLANES

        def body(j, carry):
            hf, pf, hr, pr = carry
            rows_f = pl.ds(base + j, SUBLANES, stride=step)
            af = a0[rows_f, :]
            hf = af * hf + b0[rows_f, :]
            pf = af * pf
            h0[rows_f, :] = hf
            p0[rows_f, :] = pf
            rows_r = pl.ds(base + step - 1 - j, SUBLANES, stride=step)
            ar = a1[rows_r, :]
            hr = ar * hr + b1[rows_r, :]
            pr = ar * pr
            h1[rows_r, :] = hr
            p1[rows_r, :] = pr
            return hf, pf, hr, pr

        zeros = jnp.zeros((SUBLANES, c), F32)
        ones = jnp.ones((SUBLANES, c), F32)
        lax.fori_loop(0, step, body, (zeros, ones, zeros, ones))

        carry_f = init_f
        for s in range(SUBLANES):
            rows = slice(base + s * step, base + (s + 1) * step)
            seg = h0[rows, :] + p0[rows, :] * carry_f
            h0[rows, :] = seg
            carry_f = seg[step - 1:step, :]
        carry_r = init_r
        for s in reversed(range(SUBLANES)):
            rows = slice(base + s * step, base + (s + 1) * step)
            seg = h1[rows, :] + p1[rows, :] * carry_r
            h1[rows, :] = seg
            carry_r = seg[0:1, :]
        return carry_f, carry_r

    zero_state = jnp.zeros((1, c), F32)
    state_f, state_r = scan_segment(0, n_ctx, zero_state, zero_state)
    scan_segment(n_ctx, n_lat, state_f, state_r)
    ya_ref[0] = ((h0[...] + h1[...]) * jax.nn.gelu(gr_ref[0])).astype(BF16)


def _rglru(xr, gr, conv_w, conv_b, wr, br, wi, bi, lam, n_ctx):
    b, t, d = xr.shape
    c = RNN_BLOCK
    seq = pl.BlockSpec((1, t, c), lambda bb, g: (bb, 0, g))
    vec = lambda rows: pl.BlockSpec((rows, c), lambda bb, g: (0, g))
    mat = pl.BlockSpec((2, 1, c, c), lambda bb, g: (0, g, 0, 0))
    return pl.pallas_call(
        functools.partial(_rglru_kernel, n_ctx=n_ctx),
        grid=(b, d // c),
        in_specs=[seq, seq, vec(CONV_W), vec(1), mat, vec(2), mat, vec(2), vec(2)],
        out_specs=seq,
        out_shape=jax.ShapeDtypeStruct((b, t, d), BF16),
        scratch_shapes=[pltpu.VMEM((t, c), F32)] * 8,
        compiler_params=_params(2),
        name="rglru",
    )(xr, gr, conv_w, conv_b, wr, br, wi, bi, lam)


def _fourier_kernel(al_ref, ac_ref, xc_ref, xs_ref, y_ref, *, n_ctx, scale_ctx, scale_lat):
    i = pl.program_id(1)
    n_lat = xc_ref.shape[1] - n_ctx

    def mix(a_ref, lo, n, scale):
        y = jnp.dot(a_ref[:, :n], xc_ref[0, lo:lo + n, :], preferred_element_type=F32)
        y = y + jnp.dot(a_ref[:, n:], xs_ref[0, lo:lo + n, :], preferred_element_type=F32)
        y_ref[0] = (y * scale).astype(BF16)

    @pl.when(i == 0)
    def _():
        mix(ac_ref, 0, n_ctx, scale_ctx)

    @pl.when(i > 0)
    def _():
        mix(al_ref, n_ctx, n_lat, scale_lat)


def _fourier(a_lat, a_ctx, xc, xs, n_ctx):
    b, t, d = xc.shape
    n_lat = t - n_ctx
    assert n_ctx == TM
    slab = pl.BlockSpec((1, t, d), lambda bb, i: (bb, 0, 0))
    return pl.pallas_call(
        functools.partial(_fourier_kernel, n_ctx=n_ctx,
                          scale_ctx=(n_ctx * FOURIER_GD) ** -0.5,
                          scale_lat=(n_lat * FOURIER_GD) ** -0.5),
        grid=(b, t // TM),
        in_specs=[
            pl.BlockSpec((TM, 2 * n_lat), lambda bb, i: (jnp.maximum(i - 1, 0), 0)),
            _const_spec(a_ctx.shape),
            slab, slab,
        ],
        out_specs=pl.BlockSpec((1, TM, d), lambda bb, i: (bb, i, 0)),
        out_shape=jax.ShapeDtypeStruct((b, t, d), BF16),
        compiler_params=_params(2),
        name="fourier",
    )(a_lat, a_ctx, xc, xs)


def _attn_kernel(lam_ref, q_ref, k_ref, v_ref, o_ref, s_scr, *, n_ctx, lam_init):
    i = pl.program_id(2)
    lv = lam_ref[...]
    lam = (jnp.exp(jnp.sum(lv[0:1] * lv[1:2], axis=1, keepdims=True))
           - jnp.exp(jnp.sum(lv[2:3] * lv[3:4], axis=1, keepdims=True)) + lam_init)
    q = q_ref[0]
    lane = lax.broadcasted_iota(jnp.int32, q.shape, 1)
    zero = jnp.zeros_like(q)
    qq = jnp.concatenate([jnp.where(lane < HEAD_DIM, q, zero), jnp.where(lane >= HEAD_DIM, q, zero)], axis=0)
    tq = q.shape[0]

    def attend(n_keys):
        halves = (slice(0, tq), slice(tq, 2 * tq))
        m_el = [None, None]
        for c0 in range(0, n_keys, KEY_CHUNK):
            for hf, rows in enumerate(halves):
                s = lax.dot_general(qq[rows], k_ref[0, c0:c0 + KEY_CHUNK, :], (((1,), (1,)), ((), ())),
                                    preferred_element_type=F32)
                s_scr[rows, c0:c0 + KEY_CHUNK] = s
                for l0 in range(0, KEY_CHUNK, LANES):
                    sl = s[:, l0:l0 + LANES]
                    m_el[hf] = sl if m_el[hf] is None else jnp.maximum(m_el[hf], sl)
        m = [jnp.max(me, axis=-1, keepdims=True) for me in m_el]
        u = [jnp.zeros((tq, V_EXT), F32), jnp.zeros((tq, V_EXT), F32)]
        for c0 in range(0, n_keys, KEY_CHUNK):
            for hf, rows in enumerate(halves):
                p = jnp.exp2(s_scr[rows, c0:c0 + KEY_CHUNK] - m[hf]).astype(BF16)
                u[hf] = u[hf] + jnp.dot(p, v_ref[0, c0:c0 + KEY_CHUNK, :], preferred_element_type=F32)
        o = u[0][:, :V_DIM] * (1.0 / u[0][:, V_DIM:]) - u[1][:, :V_DIM] * (lam / u[1][:, V_DIM:])
        o_ref[0] = (_rms(o) * (1.0 - lam_init)).astype(BF16)

    @pl.when(i == 0)
    def _():
        attend(n_ctx)

    @pl.when(i > 0)
    def _():
        attend(k_ref.shape[1])


def _attention(lam_vec, q, k, v_ext, n_ctx, lam_init):
    b, t, d = q.shape
    assert n_ctx == TM
    keys = pl.BlockSpec((1, t, V_DIM), lambda bb, hh, i: (bb, 0, hh))
    vals = pl.BlockSpec((1, t, V_EXT), lambda bb, hh, i: (bb, 0, hh))
    qo = pl.BlockSpec((1, TM, V_DIM), lambda bb, hh, i: (bb, i, hh))
    return pl.pallas_call(
        functools.partial(_attn_kernel, n_ctx=n_ctx, lam_init=lam_init),
        grid=(b, d // V_DIM, t // TM),
        in_specs=[pl.BlockSpec(lam_vec.shape, lambda bb, hh, i: (0, 0)), qo, keys, vals],
        out_specs=qo,
        out_shape=jax.ShapeDtypeStruct((b, t, d), BF16),
        scratch_shapes=[pltpu.VMEM((2 * TM, t), F32)],
        compiler_params=_params(3),
        name="diffattn",
    )(lam_vec, q, k, v_ext)


def _merge_kernel(x_ref, mod_ref, h_ref, ya_ref, yb_ref, yc_ref, wb_ref, wg_ref, bg_ref, wo_ref, o_ref):
    d = x_ref.shape[-1]
    g = jax.nn.sigmoid(jnp.dot(h_ref[0], wg_ref[...], preferred_element_type=F32) + bg_ref[...])
    m = g[:, :d] * jnp.dot(ya_ref[0], wb_ref[:d, :], preferred_element_type=F32)
    m = m + g[:, d:2 * d] * jnp.dot(yb_ref[0], wb_ref[d:2 * d, :], preferred_element_type=F32)
    m = m + g[:, 2 * d:] * jnp.dot(yc_ref[0], wb_ref[2 * d:, :], preferred_element_type=F32)
    out = jnp.dot(m.astype(BF16), wo_ref[...], preferred_element_type=F32)
    o_ref[0] = x_ref[0] + mod_ref[0, 2:3, :] * out


def _merge(x_all, mod, h, ya, yb, yc, w_branch, w_gate, b_gate, w_out, tile0):
    b, t, d = x_all.shape
    nt = t // TM - tile0
    tok = pl.BlockSpec((1, TM, d), lambda bb, i: (bb, i + tile0, 0))
    return pl.pallas_call(
        _merge_kernel,
        grid=(b, nt),
        in_specs=[
            tok,
            pl.BlockSpec((1, N_MOD, d), lambda bb, i: (jnp.where(i + tile0 == 0, b, bb), 0, 0)),
            tok, tok, tok, tok,
            _const_spec(w_branch.shape), _const_spec(w_gate.shape), _const_spec(b_gate.shape),
            _const_spec(w_out.shape),
        ],
        out_specs=pl.BlockSpec((1, TM, d), lambda bb, i: (bb, i, 0)),
        out_shape=jax.ShapeDtypeStruct((b, nt * TM, d), F32),
        compiler_params=_params(2),
        name="merge",
    )(x_all, mod, h, ya, yb, yc, w_branch, w_gate, b_gate, w_out)


def _ffn_kernel(x_ref, mod_ref, w1_ref, w3_ref, w2_ref, fg_ref, o_ref, *, final_norm):
    x = x_ref[0]
    hb = (_rms(x) * (1.0 + mod_ref[0, 4:5, :]) + mod_ref[0, 3:4, :]).astype(BF16)
    a = jnp.dot(hb, w1_ref[...], preferred_element_type=F32)
    g = jnp.dot(hb, w3_ref[...], preferred_element_type=F32)
    s = (a * jax.nn.sigmoid(a) * g).astype(BF16)
    y = x + mod_ref[0, 5:6, :] * jnp.dot(s, w2_ref[...], preferred_element_type=F32)
    if final_norm:
        y = _rms(y) * fg_ref[...]
    o_ref[0] = y


def _ffn(x_all, mod, w1, w3, w2, final_g, ctx_tiles, final_norm):
    b, t, d = x_all.shape
    tok = pl.BlockSpec((1, TM, d), lambda bb, i: (bb, i, 0))
    return pl.pallas_call(
        functools.partial(_ffn_kernel, final_norm=final_norm),
        grid=(b, t // TM),
        in_specs=[
            tok,
            pl.BlockSpec((1, N_MOD, d), lambda bb, i: (jnp.where(i < ctx_tiles, b, bb), 0, 0)),
            _const_spec(w1.shape), _const_spec(w3.shape), _const_spec(w2.shape),
            _const_spec(final_g.shape),
        ],
        out_specs=tok,
        out_shape=jax.ShapeDtypeStruct((b, t, d), F32),
        compiler_params=_params(2),
        name="ffn",
    )(x_all, mod, w1, w3, w2, final_g)


def _rope_tables(n_ctx, n_lat):
    p = jnp.arange(n_lat)
    rows = (p // GRID_W).astype(F32)
    cols = (p % GRID_W).astype(F32)
    n_freq = HEAD_DIM // 4
    inv = ROPE_BASE ** (-jnp.arange(n_freq, dtype=F32) / n_freq)
    ang = jnp.concatenate([rows[:, None] * inv, cols[:, None] * inv], axis=-1)
    cos, sin = jnp.cos(ang), jnp.sin(ang)
    reps = LANES // HEAD_DIM
    cos_l = jnp.tile(jnp.concatenate([cos, cos], axis=-1), (1, reps))
    sin_l = jnp.tile(jnp.concatenate([-sin, sin], axis=-1), (1, reps))
    cos_t = jnp.concatenate([jnp.ones((n_ctx, LANES), F32), cos_l], axis=0)
    sin_t = jnp.concatenate([jnp.zeros((n_ctx, LANES), F32), sin_l], axis=0)
    return cos_t, sin_t


def _dft_cos_sin(n):
    k = jnp.arange(n, dtype=jnp.int32)
    ang = ((k[:, None] * k[None, :]) % n).astype(F32) * (2.0 * math.pi / n)
    return jnp.cos(ang), jnp.sin(ang)


def _dft_position_matrix(n):
    c, s = _dft_cos_sin(n)
    return jnp.concatenate([c, -s], axis=1).astype(BF16)


def kernel(x, c, ctx, c_ctx, ada_w, ada_b, w_in, rnn_conv_w, rnn_conv_b, rnn_wr, rnn_br, rnn_wi, rnn_bi,
           rnn_lambda, attn_lambda, w_branch, w_gate, b_gate, w_out, ffn_w1, ffn_w3, ffn_w2, final_g):
    b, n_lat, d = x.shape
    n_ctx = ctx.shape[1]
    depth = ada_w.shape[0]
    ctx_tiles = n_ctx // TM

    cond = jnp.zeros((2 * SUBLANES, d), F32).at[:b].set(c).at[b].set(c_ctx)
    mod_all = _adaln(cond, ada_w, ada_b.reshape(depth, 1, N_MOD * d))

    cos_t, sin_t = _rope_tables(n_ctx, n_lat)
    cc, sc = _dft_cos_sin(FOURIER_GD)
    wf = jnp.concatenate([cc, sc], axis=1).astype(BF16)
    a_lat = _dft_position_matrix(n_lat)
    a_ctx = _dft_position_matrix(n_ctx)
    fg = final_g.reshape(1, d)

    x_all = jnp.concatenate([ctx, x], axis=1)
    for l in range(depth):
        last = l == depth - 1
        lam_init = 0.8 - 0.6 * math.exp(-0.3 * l)
        mod = mod_all[l].reshape(2 * SUBLANES, N_MOD, d)
        h, xr, gr, xc, xs, q, k, v_ext = _inproj(x_all, mod, w_in[l].astype(BF16), wf, cos_t, sin_t)
        ya = _rglru(xr, gr, rnn_conv_w[l], rnn_conv_b[l].reshape(1, d), rnn_wr[l].astype(BF16), rnn_br[l],
                    rnn_wi[l].astype(BF16), rnn_bi[l], rnn_lambda[l], n_ctx)
        yb = _fourier(a_lat, a_ctx, xc, xs, n_ctx)
        yc = _attention(attn_lambda[l], q, k, v_ext, n_ctx, lam_init)
        tile0 = ctx_tiles if last else 0
        x_mid = _merge(x_all, mod, h, ya, yb, yc, w_branch[l].astype(BF16), w_gate[l].astype(BF16),
                       b_gate[l].reshape(1, -1), w_out[l].astype(BF16), tile0)
        x_all = _ffn(x_mid, mod, ffn_w1[l].astype(BF16), ffn_w3[l].astype(BF16), ffn_w2[l].astype(BF16),
                     fg, ctx_tiles - tile0, last)
    return x_all
```

```python
import functools
import math

import jax
import jax.numpy as jnp
from jax import lax
from jax.experimental import pallas as pl
from jax.experimental.pallas import tpu as pltpu

F32 = jnp.float32
BF16 = jnp.bfloat16

EPS = 1e-6
N_MOD = 6
GRID_W = 64
RNN_BLOCK = 128
CONV_W = 4
LRU_C = 8.0
FOURIER_GD = 128
N_HEADS = 8
HEAD_DIM = 64
V_DIM = 2 * HEAD_DIM
ROPE_BASE = 10000.0

LANES = 128
SUBLANES = 8
TM = 256
SCAN_UNROLL = 8
KEY_CHUNK = 256
V_EXT = 2 * V_DIM
LOG2_E = math.log2(math.e)
VMEM_LIMIT = 56 * 1024 * 1024


def _params(n_axes):
    return pltpu.CompilerParams(
        dimension_semantics=("parallel",) * n_axes, vmem_limit_bytes=VMEM_LIMIT)


def _const_spec(shape):
    nd = len(shape)
    return pl.BlockSpec(shape, lambda *_: (0,) * nd, pipeline_mode=pl.Buffered(1))


def _rms(x):
    return x * lax.rsqrt(jnp.mean(x * x, axis=-1, keepdims=True) + EPS)


def _adaln_kernel(c_ref, w_ref, b_ref, o_ref):
    c = c_ref[...]
    s = (c * jax.nn.sigmoid(c)).astype(BF16)
    o_ref[0] = jnp.dot(s, w_ref[0].astype(BF16), preferred_element_type=F32) + b_ref[0]


def _adaln(cond, ada_w, ada_b):
    depth, d, n = ada_w.shape
    r = cond.shape[0]
    tn = 1536
    return pl.pallas_call(
        _adaln_kernel,
        grid=(depth, n // tn),
        in_specs=[
            pl.BlockSpec((r, d), lambda l, j: (0, 0)),
            pl.BlockSpec((1, d, tn), lambda l, j: (l, 0, j)),
            pl.BlockSpec((1, 1, tn), lambda l, j: (l, 0, j)),
        ],
        out_specs=pl.BlockSpec((1, r, tn), lambda l, j: (l, 0, j)),
        out_shape=jax.ShapeDtypeStruct((depth, r, n), F32),
        compiler_params=_params(2),
        name="adaln",
    )(cond, ada_w, ada_b)


def _inproj_kernel(x_ref, mod_ref, w_ref, wf_ref, cos_ref, sin_ref,
                   h_ref, xr_ref, gr_ref, xc_ref, xs_ref, q_ref, k_ref, v_ref):
    d = x_ref.shape[-1]
    x = x_ref[0]
    h = _rms(x) * (1.0 + mod_ref[0, 1:2, :]) + mod_ref[0, 0:1, :]
    hb = h.astype(BF16)
    h_ref[0] = hb

    def proj(j):
        return jnp.dot(hb, w_ref[:, j * d:(j + 1) * d], preferred_element_type=F32)

    xr_ref[0] = proj(0)
    gr_ref[0] = proj(1)

    xf = proj(2).astype(BF16)
    for g in range(d // FOURIER_GD):
        sl = slice(g * FOURIER_GD, (g + 1) * FOURIER_GD)
        z = jnp.dot(xf[:, sl], wf_ref[...], preferred_element_type=F32)
        xc_ref[0, :, sl] = z[:, :FOURIER_GD].astype(BF16)
        xs_ref[0, :, sl] = z[:, FOURIER_GD:].astype(BF16)

    cos = cos_ref[...]
    sin = sin_ref[...]
    lane = lax.broadcasted_iota(jnp.int32, cos.shape, 1)
    first_half = (lane % HEAD_DIM) < (HEAD_DIM // 2)

    def rope(t, o_ref, scale):
        for hh in range(d // LANES):
            sl = slice(hh * LANES, (hh + 1) * LANES)
            th = t[:, sl]
            swapped = jnp.where(first_half,
                                pltpu.roll(th, LANES - HEAD_DIM // 2, 1),
                                pltpu.roll(th, HEAD_DIM // 2, 1))
            o_ref[0, :, sl] = ((th * cos + swapped * sin) * scale).astype(BF16)

    rope(proj(3), q_ref, HEAD_DIM ** -0.5 * LOG2_E)
    rope(proj(4), k_ref, 1.0)

    v = proj(5).astype(BF16)
    ones = jnp.ones((v.shape[0], V_EXT - V_DIM), BF16)
    for hh in range(d // V_DIM):
        v_ref[0, :, hh * V_EXT:hh * V_EXT + V_DIM] = v[:, hh * V_DIM:(hh + 1) * V_DIM]
        v_ref[0, :, hh * V_EXT + V_DIM:(hh + 1) * V_EXT] = ones


def _piece_block(i, nt):
    return jnp.where(i == 0, nt - 1, i - 1)


def _inproj(x_all, mod, w_in, wf, cos_t, sin_t):
    b, t, d = x_all.shape
    nt = t // TM
    d_ext = d // V_DIM * V_EXT
    tok = pl.BlockSpec((1, TM, d), lambda bb, i: (bb, i, 0))
    piece = pl.BlockSpec((1, TM, d), lambda bb, i: (bb, 0, _piece_block(i, nt)))
    f32_out = jax.ShapeDtypeStruct((b, TM, nt * d), F32)
    bf_out = jax.ShapeDtypeStruct((b, t, d), BF16)
    return pl.pallas_call(
        _inproj_kernel,
        grid=(b, nt),
        in_specs=[
            tok,
            pl.BlockSpec((1, N_MOD, d), lambda bb, i: (jnp.where(i == 0, b, bb), 0, 0)),
            _const_spec(w_in.shape),
            _const_spec(wf.shape),
            pl.BlockSpec((TM, LANES), lambda bb, i: (i, 0)),
            pl.BlockSpec((TM, LANES), lambda bb, i: (i, 0)),
        ],
        out_specs=[tok, piece, piece] + [tok] * 4 + [pl.BlockSpec((1, TM, d_ext), lambda bb, i: (bb, i, 0))],
        out_shape=[bf_out, f32_out, f32_out, bf_out, bf_out, bf_out, bf_out,
                   jax.ShapeDtypeStruct((b, t, d_ext), BF16)],
        compiler_params=_params(2),
        name="inproj",
    )(x_all, mod, w_in, wf, cos_t, sin_t)


def _log_sigmoid(x):
    return jnp.minimum(x, 0.0) - jnp.log(1.0 + jnp.exp(-jnp.abs(x)))


def _gate_coeffs(u, dd, wr_ref, br_ref, wi_ref, bi_ref, lam_ref):
    ub = u.astype(BF16)
    r = jax.nn.sigmoid(jnp.dot(ub, wr_ref[dd, 0], preferred_element_type=F32) + br_ref[dd:dd + 1, :])
    gate_i = jax.nn.sigmoid(jnp.dot(ub, wi_ref[dd, 0], preferred_element_type=F32) + bi_ref[dd:dd + 1, :])
    a = jnp.exp(LRU_C * r * _log_sigmoid(lam_ref[dd:dd + 1, :]))
    return a, jnp.sqrt(1.0 - a * a) * gate_i * u


def _scan_rows(a0, b0, a1, b1, h0, p0, h1, p1):
    n, c = a0.shape
    step = n // SUBLANES

    def body(j, carry):
        hf, pf, hr, pr = carry
        rows_f = pl.ds(j, SUBLANES, stride=step)
        af = a0[rows_f, :]
        hf = af * hf + b0[rows_f, :]
        pf = af * pf
        h0[rows_f, :] = hf
        p0[rows_f, :] = pf
        rows_r = pl.ds(step - 1 - j, SUBLANES, stride=step)
        ar = a1[rows_r, :]
        hr = ar * hr + b1[rows_r, :]
        pr = ar * pr
        h1[rows_r, :] = hr
        p1[rows_r, :] = pr
        return hf, pf, hr, pr

    zeros = jnp.zeros((SUBLANES, c), F32)
    ones = jnp.ones((SUBLANES, c), F32)
    lax.fori_loop(0, step, body, (zeros, ones, zeros, ones))

    carry_f = jnp.zeros((1, c), F32)
    for s in range(SUBLANES):
        rows = slice(s * step, (s + 1) * step)
        seg = h0[rows, :] + p0[rows, :] * carry_f
        h0[rows, :] = seg
        carry_f = seg[step - 1:step, :]
    carry_r = jnp.zeros((1, c), F32)
    for s in reversed(range(SUBLANES)):
        rows = slice(s * step, (s + 1) * step)
        seg = h1[rows, :] + p1[rows, :] * carry_r
        h1[rows, :] = seg
        carry_r = seg[0:1, :]
    return carry_f, carry_r


def _rglru_kernel(xl_ref, xc_ref, gl_ref, gc_ref, cw_ref, cb_ref, wr_ref, br_ref, wi_ref, bi_ref, lam_ref,
                  yl_ref, yc_ref, xpad, al0, bl0, al1, bl1, hl0, hl1,
                  ac0, bc0, ac1, bc1, hc0, pc0, hc1, pc1):
    n_ctx, c = xc_ref.shape[1], xc_ref.shape[2]
    step, n_piece = xl_ref.shape[1], xl_ref.shape[2]
    gates = functools.partial(_gate_coeffs, wr_ref=wr_ref, br_ref=br_ref, wi_ref=wi_ref, bi_ref=bi_ref,
                              lam_ref=lam_ref)
    w = [cw_ref[k:k + 1, :] for k in range(CONV_W)]

    x = xc_ref[0]
    t = lax.broadcasted_iota(jnp.int32, (n_ctx, c), 0)
    u = jnp.where(t >= 2, pltpu.roll(x, 2, 0), 0.0) * w[0]
    u = u + jnp.where(t >= 1, pltpu.roll(x, 1, 0), 0.0) * w[1]
    u = u + x * w[2]
    u = u + jnp.where(t < n_ctx - 1, pltpu.roll(x, n_ctx - 1, 0), 0.0) * w[3]
    u = u + cb_ref[...]
    ac0[...], bc0[...] = gates(u, 0)
    ac1[...], bc1[...] = gates(u, 1)
    state_f, state_r = _scan_rows(ac0, bc0, ac1, bc1, hc0, pc0, hc1, pc1)
    yc_ref[0] = (hc0[...] + hc1[...]) * jax.nn.gelu(gc_ref[0])

    sub = lax.broadcasted_iota(jnp.int32, (n_piece, c), 0)

    def from_prev_piece(v):
        return jnp.where(sub >= 1, pltpu.roll(v, 1, 0), 0.0)

    def from_next_piece(v):
        return jnp.where(sub < n_piece - 1, pltpu.roll(v, n_piece - 1, 0), 0.0)

    xpad[2:step + 2] = xl_ref[0]
    xpad[0] = from_prev_piece(xl_ref[0, step - 2])
    xpad[1] = from_prev_piece(xl_ref[0, step - 1])
    xpad[step + 2] = from_next_piece(xl_ref[0, 0])
    u = xpad[0:step] * w[0]
    u = u + xpad[1:step + 1] * w[1]
    u = u + xpad[2:step + 2] * w[2]
    u = u + xpad[3:step + 3] * w[3]
    u = (u + cb_ref[...]).reshape(step * n_piece, c)
    for dd, (a_s, b_s) in enumerate(((al0, bl0), (al1, bl1))):
        a, bb = gates(u, dd)
        a_s[...] = a.reshape(step, n_piece, c)
        b_s[...] = bb.reshape(step, n_piece, c)

    def piece_finals(j, carry):
        hf, pf, hr, pr = carry
        af = al0[j]
        ar = al1[step - 1 - j]
        return af * hf + bl0[j], af * pf, ar * hr + bl1[step - 1 - j], ar * pr

    zeros = jnp.zeros((n_piece, c), F32)
    ones = jnp.ones((n_piece, c), F32)
    hf, pf, hr, pr = lax.fori_loop(0, step, piece_finals, (zeros, ones, zeros, ones), unroll=SCAN_UNROLL)

    init_f = zeros
    state = state_f
    for s in range(n_piece):
        init_f = jnp.where(sub == s, state, init_f)
        state = hf[s:s + 1, :] + pf[s:s + 1, :] * state
    init_r = zeros
    state = state_r
    for s in reversed(range(n_piece)):
        init_r = jnp.where(sub == s, state, init_r)
        state = hr[s:s + 1, :] + pr[s:s + 1, :] * state

    def piece_states(j, carry):
        hf, hr = carry
        hf = al0[j] * hf + bl0[j]
        hl0[j] = hf
        hr = al1[step - 1 - j] * hr + bl1[step - 1 - j]
        hl1[step - 1 - j] = hr
        return hf, hr

    lax.fori_loop(0, step, piece_states, (init_f, init_r), unroll=SCAN_UNROLL)
    yl_ref[0] = (hl0[...] + hl1[...]) * jax.nn.gelu(gl_ref[0])


def _rglru(xr_p, gr_p, conv_w, conv_b, wr, br, wi, bi, lam, n_piece):
    b, step, width = xr_p.shape
    d = width // (n_piece + 1)
    c = RNN_BLOCK
    as_pieces = lambda v: v.reshape(b, step, n_piece + 1, d)
    lat = pl.BlockSpec((1, step, n_piece, c), lambda bb, g: (bb, 0, 0, g))
    ctx_in = pl.BlockSpec((1, step, c), lambda bb, g: (bb, 0, n_piece * (d // c) + g))
    ctx_out = pl.BlockSpec((1, step, c), lambda bb, g: (bb, 0, g))
    vec = lambda rows: pl.BlockSpec((rows, c), lambda bb, g: (0, g))
    mat = pl.BlockSpec((2, 1, c, c), lambda bb, g: (0, g, 0, 0))
    return pl.pallas_call(
        _rglru_kernel,
        grid=(b, d // c),
        in_specs=[lat, ctx_in, lat, ctx_in, vec(CONV_W), vec(1), mat, vec(2), mat, vec(2), vec(2)],
        out_specs=[lat, ctx_out],
        out_shape=[jax.ShapeDtypeStruct((b, step, n_piece, d), F32), jax.ShapeDtypeStruct((b, step, d), F32)],
        scratch_shapes=([pltpu.VMEM((step + CONV_W - 1, n_piece, c), F32)]
                        + [pltpu.VMEM((step, n_piece, c), F32)] * 6
                        + [pltpu.VMEM((step, c), F32)] * 8),
        compiler_params=_params(2),
        name="rglru",
    )(as_pieces(xr_p), xr_p, as_pieces(gr_p), gr_p, conv_w, conv_b, wr, br, wi, bi, lam)


def _fourier_kernel(al_ref, ac_ref, xc_ref, xs_ref, y_ref, *, n_ctx, scale_ctx, scale_lat):
    i = pl.program_id(1)
    n_lat = xc_ref.shape[1] - n_ctx

    def mix(a_ref, lo, n, scale):
        y = jnp.dot(a_ref[:, :n], xc_ref[0, lo:lo + n, :], preferred_element_type=F32)
        y = y + jnp.dot(a_ref[:, n:], xs_ref[0, lo:lo + n, :], preferred_element_type=F32)
        y_ref[0] = (y * scale).astype(BF16)

    @pl.when(i == 0)
    def _():
        mix(ac_ref, 0, n_ctx, scale_ctx)

    @pl.when(i > 0)
    def _():
        mix(al_ref, n_ctx, n_lat, scale_lat)


def _fourier(a_lat, a_ctx, xc, xs, n_ctx):
    b, t, d = xc.shape
    n_lat = t - n_ctx
    assert n_ctx == TM
    slab = pl.BlockSpec((1, t, d), lambda bb, i: (bb, 0, 0))
    return pl.pallas_call(
        functools.partial(_fourier_kernel, n_ctx=n_ctx,
                          scale_ctx=(n_ctx * FOURIER_GD) ** -0.5,
                          scale_lat=(n_lat * FOURIER_GD) ** -0.5),
        grid=(b, t // TM),
        in_specs=[
            pl.BlockSpec((TM, 2 * n_lat), lambda bb, i: (jnp.maximum(i - 1, 0), 0)),
            _const_spec(a_ctx.shape),
            slab, slab,
        ],
        out_specs=pl.BlockSpec((1, TM, d), lambda bb, i: (bb, i, 0)),
        out_shape=jax.ShapeDtypeStruct((b, t, d), BF16),
        compiler_params=_params(2),
        name="fourier",
    )(a_lat, a_ctx, xc, xs)


def _attn_kernel(lam_ref, q_ref, k_ref, v_ref, o_ref, s_scr, *, n_ctx, lam_init):
    i = pl.program_id(2)
    lv = lam_ref[...]
    lam = (jnp.exp(jnp.sum(lv[0:1] * lv[1:2], axis=1, keepdims=True))
           - jnp.exp(jnp.sum(lv[2:3] * lv[3:4], axis=1, keepdims=True)) + lam_init)
    q = q_ref[0]
    lane = lax.broadcasted_iota(jnp.int32, q.shape, 1)
    zero = jnp.zeros_like(q)
    q_halves = (jnp.where(lane < HEAD_DIM, q, zero), jnp.where(lane >= HEAD_DIM, q, zero))
    tq = q.shape[0]
    half_rows = (slice(0, tq), slice(tq, 2 * tq))

    def attend(n_keys):
        m_el = [None, None]
        for c0 in range(0, n_keys, KEY_CHUNK):
            for hf, rows in enumerate(half_rows):
                s = lax.dot_general(q_halves[hf], k_ref[0, c0:c0 + KEY_CHUNK, :], (((1,), (1,)), ((), ())),
                                    preferred_element_type=F32)
                s_scr[rows, c0:c0 + KEY_CHUNK] = s
                for l0 in range(0, KEY_CHUNK, LANES):
                    sl = s[:, l0:l0 + LANES]
                    m_el[hf] = sl if m_el[hf] is None else jnp.maximum(m_el[hf], sl)
        m = [jnp.max(me, axis=-1, keepdims=True) for me in m_el]
        u = [jnp.zeros((tq, V_EXT), F32), jnp.zeros((tq, V_EXT), F32)]
        for c0 in range(0, n_keys, KEY_CHUNK):
            for hf, rows in enumerate(half_rows):
                p = jnp.exp2(s_scr[rows, c0:c0 + KEY_CHUNK] - m[hf]).astype(BF16)
                u[hf] = u[hf] + jnp.dot(p, v_ref[0, c0:c0 + KEY_CHUNK, :], preferred_element_type=F32)
        o = u[0][:, :V_DIM] * (1.0 / u[0][:, V_DIM:]) - u[1][:, :V_DIM] * (lam / u[1][:, V_DIM:])
        o_ref[0] = (_rms(o) * (1.0 - lam_init)).astype(BF16)

    @pl.when(i == 0)
    def _():
        attend(n_ctx)

    @pl.when(i > 0)
    def _():
        attend(k_ref.shape[1])


def _attention(lam_vec, q, k, v_ext, n_ctx, lam_init):
    b, t, d = q.shape
    assert n_ctx == TM
    keys = pl.BlockSpec((1, t, V_DIM), lambda bb, hh, i: (bb, 0, hh))
    vals = pl.BlockSpec((1, t, V_EXT), lambda bb, hh, i: (bb, 0, hh))
    qo = pl.BlockSpec((1, TM, V_DIM), lambda bb, hh, i: (bb, i, hh))
    return pl.pallas_call(
        functools.partial(_attn_kernel, n_ctx=n_ctx, lam_init=lam_init),
        grid=(b, d // V_DIM, t // TM),
        in_specs=[pl.BlockSpec(lam_vec.shape, lambda bb, hh, i: (0, 0)), qo, keys, vals],
        out_specs=qo,
        out_shape=jax.ShapeDtypeStruct((b, t, d), BF16),
        scratch_shapes=[pltpu.VMEM((2 * TM, t), F32)],
        compiler_params=_params(3),
        name="diffattn",
    )(lam_vec, q, k, v_ext)


def _merge_kernel(x_ref, mod_ref, h_ref, yal_ref, yac_ref, yb_ref, yc_ref, wb_ref, wg_ref, bg_ref, wo_ref,
                  o_ref, *, tile0):
    d = x_ref.shape[-1]
    g = jax.nn.sigmoid(jnp.dot(h_ref[0], wg_ref[...], preferred_element_type=F32) + bg_ref[...])
    ya = jnp.where(pl.program_id(1) + tile0 == 0, yac_ref[0], yal_ref[0]).astype(BF16)
    m = g[:, :d] * jnp.dot(ya, wb_ref[:d, :], preferred_element_type=F32)
    m = m + g[:, d:2 * d] * jnp.dot(yb_ref[0], wb_ref[d:2 * d, :], preferred_element_type=F32)
    m = m + g[:, 2 * d:] * jnp.dot(yc_ref[0], wb_ref[2 * d:, :], preferred_element_type=F32)
    out = jnp.dot(m.astype(BF16), wo_ref[...], preferred_element_type=F32)
    o_ref[0] = x_ref[0] + mod_ref[0, 2:3, :] * out


def _merge(x_all, mod, h, ya_lat, ya_ctx, yb, yc, w_branch, w_gate, b_gate, w_out, tile0):
    b, t, d = x_all.shape
    nt = t // TM - tile0
    tok = pl.BlockSpec((1, TM, d), lambda bb, i: (bb, i + tile0, 0))
    n_piece = ya_lat.shape[2]
    ya_lat = ya_lat.reshape(b, TM, n_piece * d)
    return pl.pallas_call(
        functools.partial(_merge_kernel, tile0=tile0),
        grid=(b, nt),
        in_specs=[
            tok,
            pl.BlockSpec((1, N_MOD, d), lambda bb, i: (jnp.where(i + tile0 == 0, b, bb), 0, 0)),
            tok,
            pl.BlockSpec((1, TM, d), lambda bb, i: (bb, 0, jnp.maximum(i + tile0 - 1, 0))),
            pl.BlockSpec((1, TM, d), lambda bb, i: (bb, 0, 0)),
            tok, tok,
            _const_spec(w_branch.shape), _const_spec(w_gate.shape), _const_spec(b_gate.shape),
            _const_spec(w_out.shape),
        ],
        out_specs=pl.BlockSpec((1, TM, d), lambda bb, i: (bb, i, 0)),
        out_shape=jax.ShapeDtypeStruct((b, nt * TM, d), F32),
        compiler_params=_params(2),
        name="merge",
    )(x_all, mod, h, ya_lat, ya_ctx, yb, yc, w_branch, w_gate, b_gate, w_out)


def _ffn_kernel(x_ref, mod_ref, w1_ref, w3_ref, w2_ref, fg_ref, o_ref, *, final_norm):
    x = x_ref[0]
    hb = (_rms(x) * (1.0 + mod_ref[0, 4:5, :]) + mod_ref[0, 3:4, :]).astype(BF16)
    a = jnp.dot(hb, w1_ref[...], preferred_element_type=F32)
    g = jnp.dot(hb, w3_ref[...], preferred_element_type=F32)
    s = (a * jax.nn.sigmoid(a) * g).astype(BF16)
    y = x + mod_ref[0, 5:6, :] * jnp.dot(s, w2_ref[...], preferred_element_type=F32)
    if final_norm:
        y = _rms(y) * fg_ref[...]
    o_ref[0] = y


def _ffn(x_all, mod, w1, w3, w2, final_g, ctx_tiles, final_norm):
    b, t, d = x_all.shape
    tok = pl.BlockSpec((1, TM, d), lambda bb, i: (bb, i, 0))
    return pl.pallas_call(
        functools.partial(_ffn_kernel, final_norm=final_norm),
        grid=(b, t // TM),
        in_specs=[
            tok,
            pl.BlockSpec((1, N_MOD, d), lambda bb, i: (jnp.where(i < ctx_tiles, b, bb), 0, 0)),
            _const_spec(w1.shape), _const_spec(w3.shape), _const_spec(w2.shape),
            _const_spec(final_g.shape),
        ],
        out_specs=tok,
        out_shape=jax.ShapeDtypeStruct((b, t, d), F32),
        compiler_params=_params(2),
        name="ffn",
    )(x_all, mod, w1, w3, w2, final_g)


def _rope_tables(n_ctx, n_lat):
    p = jnp.arange(n_lat)
    rows = (p // GRID_W).astype(F32)
    cols = (p % GRID_W).astype(F32)
    n_freq = HEAD_DIM // 4
    inv = ROPE_BASE ** (-jnp.arange(n_freq, dtype=F32) / n_freq)
    ang = jnp.concatenate([rows[:, None] * inv, cols[:, None] * inv], axis=-1)
    cos, sin = jnp.cos(ang), jnp.sin(ang)
    reps = LANES // HEAD_DIM
    cos_l = jnp.tile(jnp.concatenate([cos, cos], axis=-1), (1, reps))
    sin_l = jnp.tile(jnp.concatenate([-sin, sin], axis=-1), (1, reps))
    cos_t = jnp.concatenate([jnp.ones((n_ctx, LANES), F32), cos_l], axis=0)
    sin_t = jnp.concatenate([jnp.zeros((n_ctx, LANES), F32), sin_l], axis=0)
    return cos_t, sin_t


def _dft_cos_sin(n):
    k = jnp.arange(n, dtype=jnp.int32)
    ang = ((k[:, None] * k[None, :]) % n).astype(F32) * (2.0 * math.pi / n)
    return jnp.cos(ang), jnp.sin(ang)


def _dft_position_matrix(n):
    c, s = _dft_cos_sin(n)
    return jnp.concatenate([c, -s], axis=1).astype(BF16)


def kernel(x, c, ctx, c_ctx, ada_w, ada_b, w_in, rnn_conv_w, rnn_conv_b, rnn_wr, rnn_br, rnn_wi, rnn_bi,
           rnn_lambda, attn_lambda, w_branch, w_gate, b_gate, w_out, ffn_w1, ffn_w3, ffn_w2, final_g):
    b, n_lat, d = x.shape
    n_ctx = ctx.shape[1]
    depth = ada_w.shape[0]
    ctx_tiles = n_ctx // TM
    assert n_ctx == TM and n_lat == SUBLANES * TM

    cond = jnp.zeros((2 * SUBLANES, d), F32).at[:b].set(c).at[b].set(c_ctx)
    mod_all = _adaln(cond, ada_w, ada_b.reshape(depth, 1, N_MOD * d))

    cos_t, sin_t = _rope_tables(n_ctx, n_lat)
    cc, sc = _dft_cos_sin(FOURIER_GD)
    wf = jnp.concatenate([cc, sc], axis=1).astype(BF16)
    a_lat = _dft_position_matrix(n_lat)
    a_ctx = _dft_position_matrix(n_ctx)
    fg = final_g.reshape(1, d)

    x_all = jnp.concatenate([ctx, x], axis=1)
    for l in range(depth):
        last = l == depth - 1
        lam_init = 0.8 - 0.6 * math.exp(-0.3 * l)
        mod = mod_all[l].reshape(2 * SUBLANES, N_MOD, d)
        h, xr, gr, xc, xs, q, k, v_ext = _inproj(x_all, mod, w_in[l].astype(BF16), wf, cos_t, sin_t)
        ya_lat, ya_ctx = _rglru(xr, gr, rnn_conv_w[l], rnn_conv_b[l].reshape(1, d), rnn_wr[l].astype(BF16),
                                rnn_br[l], rnn_wi[l].astype(BF16), rnn_bi[l], rnn_lambda[l], SUBLANES)
        yb = _fourier(a_lat, a_ctx, xc, xs, n_ctx)
        yc = _attention(attn_lambda[l], q, k, v_ext, n_ctx, lam_init)
        tile0 = ctx_tiles if last else 0
        x_mid = _merge(x_all, mod, h, ya_lat, ya_ctx, yb, yc, w_branch[l].astype(BF16), w_gate[l].astype(BF16),
                       b_gate[l].reshape(1, -1), w_out[l].astype(BF16), tile0)
        x_all = _ffn(x_mid, mod, ffn_w1[l].astype(BF16), ffn_w3[l].astype(BF16), ffn_w2[l].astype(BF16),
                     fg, ctx_tiles - tile0, last)
    return x_all
```

```python
import functools
import math

import jax
import jax.numpy as jnp
from jax import lax
from jax.experimental import pallas as pl
from jax.experimental.pallas import tpu as pltpu

F32 = jnp.float32
BF16 = jnp.bfloat16

EPS = 1e-6
N_MOD = 6
GRID_W = 64
RNN_BLOCK = 128
CONV_W = 4
LRU_C = 8.0
FOURIER_GD = 128
N_HEADS = 8
HEAD_DIM = 64
V_DIM = 2 * HEAD_DIM
ROPE_BASE = 10000.0

LANES = 128
SUBLANES = 8
TM = 256
SCAN_UNROLL = 8
KEY_CHUNK = 256
V_EXT = 2 * V_DIM
LOG2_E = math.log2(math.e)
VMEM_LIMIT = 56 * 1024 * 1024


def _params(n_axes):
    return pltpu.CompilerParams(
        dimension_semantics=("parallel",) * n_axes, vmem_limit_bytes=VMEM_LIMIT)


def _const_spec(shape):
    nd = len(shape)
    return pl.BlockSpec(shape, lambda *_: (0,) * nd, pipeline_mode=pl.Buffered(1))


def _rms(x):
    return x * lax.rsqrt(jnp.mean(x * x, axis=-1, keepdims=True) + EPS)


def _adaln_kernel(c_ref, w_ref, b_ref, o_ref):
    c = c_ref[...]
    s = (c * jax.nn.sigmoid(c)).astype(BF16)
    o_ref[0] = jnp.dot(s, w_ref[0].astype(BF16), preferred_element_type=F32) + b_ref[0]


def _adaln(cond, ada_w, ada_b):
    depth, d, n = ada_w.shape
    r = cond.shape[0]
    tn = 1536
    return pl.pallas_call(
        _adaln_kernel,
        grid=(depth, n // tn),
        in_specs=[
            pl.BlockSpec((r, d), lambda l, j: (0, 0)),
            pl.BlockSpec((1, d, tn), lambda l, j: (l, 0, j)),
            pl.BlockSpec((1, 1, tn), lambda l, j: (l, 0, j)),
        ],
        out_specs=pl.BlockSpec((1, r, tn), lambda l, j: (l, 0, j)),
        out_shape=jax.ShapeDtypeStruct((depth, r, n), F32),
        compiler_params=_params(2),
        name="adaln",
    )(cond, ada_w, ada_b)


def _inproj_kernel(x_ref, mod_ref, w_ref, wf_ref, cos_ref, sin_ref,
                   h_ref, xr_ref, gr_ref, xc_ref, xs_ref, q_ref, k_ref, v_ref):
    d = x_ref.shape[-1]
    x = x_ref[0]
    h = _rms(x) * (1.0 + mod_ref[0, 1:2, :]) + mod_ref[0, 0:1, :]
    hb = h.astype(BF16)
    h_ref[0] = hb

    def proj(j):
        return jnp.dot(hb, w_ref[:, j * d:(j + 1) * d], preferred_element_type=F32)

    xr_ref[0] = proj(0)
    gr_ref[0] = proj(1)

    xf = proj(2).astype(BF16)
    for g in range(d // FOURIER_GD):
        sl = slice(g * FOURIER_GD, (g + 1) * FOURIER_GD)
        z = jnp.dot(xf[:, sl], wf_ref[...], preferred_element_type=F32)
        xc_ref[0, :, sl] = z[:, :FOURIER_GD].astype(BF16)
        xs_ref[0, :, sl] = z[:, FOURIER_GD:].astype(BF16)

    cos = cos_ref[...]
    sin = sin_ref[...]
    lane = lax.broadcasted_iota(jnp.int32, cos.shape, 1)
    first_half = (lane % HEAD_DIM) < (HEAD_DIM // 2)

    def rope(t, o_ref, scale):
        for hh in range(d // LANES):
            sl = slice(hh * LANES, (hh + 1) * LANES)
            th = t[:, sl]
            swapped = jnp.where(first_half,
                                pltpu.roll(th, LANES - HEAD_DIM // 2, 1),
                                pltpu.roll(th, HEAD_DIM // 2, 1))
            o_ref[0, :, sl] = ((th * cos + swapped * sin) * scale).astype(BF16)

    rope(proj(3), q_ref, HEAD_DIM ** -0.5 * LOG2_E)
    rope(proj(4), k_ref, 1.0)

    v = proj(5).astype(BF16)
    ones = jnp.ones((v.shape[0], V_EXT - V_DIM), BF16)
    for hh in range(d // V_DIM):
        v_ref[0, :, hh * V_EXT:hh * V_EXT + V_DIM] = v[:, hh * V_DIM:(hh + 1) * V_DIM]
        v_ref[0, :, hh * V_EXT + V_DIM:(hh + 1) * V_EXT] = ones


def _inproj(x_all, mod, w_in, wf, cos_t, sin_t):
    b, t, d = x_all.shape
    nt = t // TM
    d_ext = d // V_DIM * V_EXT
    tok = pl.BlockSpec((1, TM, d), lambda bb, i: (bb, i, 0))
    f32_out = jax.ShapeDtypeStruct((b, t, d), F32)
    bf_out = jax.ShapeDtypeStruct((b, t, d), BF16)
    return pl.pallas_call(
        _inproj_kernel,
        grid=(b, nt),
        in_specs=[
            tok,
            pl.BlockSpec((1, N_MOD, d), lambda bb, i: (jnp.where(i == 0, b, bb), 0, 0)),
            _const_spec(w_in.shape),
            _const_spec(wf.shape),
            pl.BlockSpec((TM, LANES), lambda bb, i: (i, 0)),
            pl.BlockSpec((TM, LANES), lambda bb, i: (i, 0)),
        ],
        out_specs=[tok] * 7 + [pl.BlockSpec((1, TM, d_ext), lambda bb, i: (bb, i, 0))],
        out_shape=[bf_out, f32_out, f32_out, bf_out, bf_out, bf_out, bf_out,
                   jax.ShapeDtypeStruct((b, t, d_ext), BF16)],
        compiler_params=_params(2),
        name="inproj",
    )(x_all, mod, w_in, wf, cos_t, sin_t)


def _log_sigmoid(x):
    return jnp.minimum(x, 0.0) - jnp.log(1.0 + jnp.exp(-jnp.abs(x)))


def _rglru_kernel(xr_ref, gr_ref, cw_ref, cb_ref, wr_ref, br_ref, wi_ref, bi_ref, lam_ref, ya_ref,
                  xpad, a0, b0, a1, b1, h0, h1, *, n_ctx):
    t_all, c = xr_ref.shape[1], xr_ref.shape[2]
    n_piece = SUBLANES
    w = [cw_ref[k:k + 1, :] for k in range(CONV_W)]
    sub = lax.broadcasted_iota(jnp.int32, (n_piece, c), 0)
    zeros = jnp.zeros((n_piece, c), F32)
    ones = jnp.ones((n_piece, c), F32)

    def from_prev_piece(v):
        return jnp.where(sub >= 1, pltpu.roll(v, 1, 0), 0.0)

    def from_next_piece(v):
        return jnp.where(sub < n_piece - 1, pltpu.roll(v, n_piece - 1, 0), 0.0)

    def run_sequence(base, n, state_f, state_r):
        steps = n // n_piece
        x3 = xr_ref[0, base:base + n, :].reshape(steps, n_piece, c)
        xpad[2:steps + 2] = x3
        xpad[0] = from_prev_piece(x3[steps - 2])
        xpad[1] = from_prev_piece(x3[steps - 1])
        xpad[steps + 2] = from_next_piece(x3[0])
        u = xpad[0:steps] * w[0]
        u = u + xpad[1:steps + 1] * w[1]
        u = u + xpad[2:steps + 2] * w[2]
        u = u + xpad[3:steps + 3] * w[3]
        u = (u + cb_ref[...]).reshape(n, c)
        ub = u.astype(BF16)
        for dd, (a_s, b_s) in enumerate(((a0, b0), (a1, b1))):
            r = jax.nn.sigmoid(jnp.dot(ub, wr_ref[dd, 0], preferred_element_type=F32) + br_ref[dd:dd + 1, :])
            gate_i = jax.nn.sigmoid(jnp.dot(ub, wi_ref[dd, 0], preferred_element_type=F32) + bi_ref[dd:dd + 1, :])
            a = jnp.exp(LRU_C * r * _log_sigmoid(lam_ref[dd:dd + 1, :]))
            a_s[0:steps] = a.reshape(steps, n_piece, c)
            b_s[0:steps] = (jnp.sqrt(1.0 - a * a) * gate_i * u).reshape(steps, n_piece, c)

        def piece_finals(j, carry):
            hf, pf, hr, pr = carry
            af = a0[j]
            ar = a1[steps - 1 - j]
            return af * hf + b0[j], af * pf, ar * hr + b1[steps - 1 - j], ar * pr

        hf, pf, hr, pr = lax.fori_loop(0, steps, piece_finals, (zeros, ones, zeros, ones), unroll=SCAN_UNROLL)

        init_f = zeros
        for s in range(n_piece):
            init_f = jnp.where(sub == s, state_f, init_f)
            state_f = hf[s:s + 1, :] + pf[s:s + 1, :] * state_f
        init_r = zeros
        for s in reversed(range(n_piece)):
            init_r = jnp.where(sub == s, state_r, init_r)
            state_r = hr[s:s + 1, :] + pr[s:s + 1, :] * state_r

        def piece_states(j, carry):
            hf, hr = carry
            hf = a0[j] * hf + b0[j]
            h0[j] = hf
            hr = a1[steps - 1 - j] * hr + b1[steps - 1 - j]
            h1[steps - 1 - j] = hr
            return hf, hr

        lax.fori_loop(0, steps, piece_states, (init_f, init_r), unroll=SCAN_UNROLL)
        h = (h0[0:steps] + h1[0:steps]).reshape(n, c)
        ya_ref[0, base:base + n, :] = (h * jax.nn.gelu(gr_ref[0, base:base + n, :])).astype(BF16)
        return state_f, state_r

    zero_state = jnp.zeros((1, c), F32)
    state_f, state_r = run_sequence(0, n_ctx, zero_state, zero_state)
    run_sequence(n_ctx, t_all - n_ctx, state_f, state_r)


def _rglru(xr, gr, conv_w, conv_b, wr, br, wi, bi, lam, n_ctx):
    b, t, d = xr.shape
    c = RNN_BLOCK
    steps = max(n_ctx, t - n_ctx) // SUBLANES
    seq = pl.BlockSpec((1, t, c), lambda bb, g: (bb, 0, g))
    vec = lambda rows: pl.BlockSpec((rows, c), lambda bb, g: (0, g))
    mat = pl.BlockSpec((2, 1, c, c), lambda bb, g: (0, g, 0, 0))
    return pl.pallas_call(
        functools.partial(_rglru_kernel, n_ctx=n_ctx),
        grid=(b, d // c),
        in_specs=[seq, seq, vec(CONV_W), vec(1), mat, vec(2), mat, vec(2), vec(2)],
        out_specs=seq,
        out_shape=jax.ShapeDtypeStruct((b, t, d), BF16),
        scratch_shapes=([pltpu.VMEM((steps + CONV_W - 1, SUBLANES, c), F32)]
                        + [pltpu.VMEM((steps, SUBLANES, c), F32)] * 6),
        compiler_params=_params(2),
        name="rglru",
    )(xr, gr, conv_w, conv_b, wr, br, wi, bi, lam)


def _fourier_kernel(al_ref, ac_ref, xc_ref, xs_ref, y_ref, *, n_ctx, scale_ctx, scale_lat):
    i = pl.program_id(1)
    n_lat = xc_ref.shape[1] - n_ctx

    def mix(a_ref, lo, n, scale):
        y = jnp.dot(a_ref[:, :n], xc_ref[0, lo:lo + n, :], preferred_element_type=F32)
        y = y + jnp.dot(a_ref[:, n:], xs_ref[0, lo:lo + n, :], preferred_element_type=F32)
        y_ref[0] = (y * scale).astype(BF16)

    @pl.when(i == 0)
    def _():
        mix(ac_ref, 0, n_ctx, scale_ctx)

    @pl.when(i > 0)
    def _():
        mix(al_ref, n_ctx, n_lat, scale_lat)


def _fourier(a_lat, a_ctx, xc, xs, n_ctx):
    b, t, d = xc.shape
    n_lat = t - n_ctx
    assert n_ctx == TM
    slab = pl.BlockSpec((1, t, d), lambda bb, i: (bb, 0, 0))
    return pl.pallas_call(
        functools.partial(_fourier_kernel, n_ctx=n_ctx,
                          scale_ctx=(n_ctx * FOURIER_GD) ** -0.5,
                          scale_lat=(n_lat * FOURIER_GD) ** -0.5),
        grid=(b, t // TM),
        in_specs=[
            pl.BlockSpec((TM, 2 * n_lat), lambda bb, i: (jnp.maximum(i - 1, 0), 0)),
            _const_spec(a_ctx.shape),
            slab, slab,
        ],
        out_specs=pl.BlockSpec((1, TM, d), lambda bb, i: (bb, i, 0)),
        out_shape=jax.ShapeDtypeStruct((b, t, d), BF16),
        compiler_params=_params(2),
        name="fourier",
    )(a_lat, a_ctx, xc, xs)


def _attn_kernel(lam_ref, q_ref, k_ref, v_ref, o_ref, s_scr, *, n_ctx, lam_init):
    i = pl.program_id(2)
    lv = lam_ref[...]
    lam = (jnp.exp(jnp.sum(lv[0:1] * lv[1:2], axis=1, keepdims=True))
           - jnp.exp(jnp.sum(lv[2:3] * lv[3:4], axis=1, keepdims=True)) + lam_init)
    q = q_ref[0]
    lane = lax.broadcasted_iota(jnp.int32, q.shape, 1)
    zero = jnp.zeros_like(q)
    q_halves = (jnp.where(lane < HEAD_DIM, q, zero), jnp.where(lane >= HEAD_DIM, q, zero))
    tq = q.shape[0]
    half_rows = (slice(0, tq), slice(tq, 2 * tq))

    def attend(n_keys):
        m_el = [None, None]
        for c0 in range(0, n_keys, KEY_CHUNK):
            for hf, rows in enumerate(half_rows):
                s = lax.dot_general(q_halves[hf], k_ref[0, c0:c0 + KEY_CHUNK, :], (((1,), (1,)), ((), ())),
                                    preferred_element_type=F32)
                s_scr[rows, c0:c0 + KEY_CHUNK] = s
                for l0 in range(0, KEY_CHUNK, LANES):
                    sl = s[:, l0:l0 + LANES]
                    m_el[hf] = sl if m_el[hf] is None else jnp.maximum(m_el[hf], sl)
        m = [jnp.max(me, axis=-1, keepdims=True) for me in m_el]
        u = [jnp.zeros((tq, V_EXT), F32), jnp.zeros((tq, V_EXT), F32)]
        for c0 in range(0, n_keys, KEY_CHUNK):
            for hf, rows in enumerate(half_rows):
                p = jnp.exp2(s_scr[rows, c0:c0 + KEY_CHUNK] - m[hf]).astype(BF16)
                u[hf] = u[hf] + jnp.dot(p, v_ref[0, c0:c0 + KEY_CHUNK, :], preferred_element_type=F32)
        o = u[0][:, :V_DIM] * (1.0 / u[0][:, V_DIM:]) - u[1][:, :V_DIM] * (lam / u[1][:, V_DIM:])
        o_ref[0] = (_rms(o) * (1.0 - lam_init)).astype(BF16)

    @pl.when(i == 0)
    def _():
        attend(n_ctx)

    @pl.when(i > 0)
    def _():
        attend(k_ref.shape[1])


def _attention(lam_vec, q, k, v_ext, n_ctx, lam_init):
    b, t, d = q.shape
    assert n_ctx == TM
    keys = pl.BlockSpec((1, t, V_DIM), lambda bb, hh, i: (bb, 0, hh))
    vals = pl.BlockSpec((1, t, V_EXT), lambda bb, hh, i: (bb, 0, hh))
    qo = pl.BlockSpec((1, TM, V_DIM), lambda bb, hh, i: (bb, i, hh))
    return pl.pallas_call(
        functools.partial(_attn_kernel, n_ctx=n_ctx, lam_init=lam_init),
        grid=(b, d // V_DIM, t // TM),
        in_specs=[pl.BlockSpec(lam_vec.shape, lambda bb, hh, i: (0, 0)), qo, keys, vals],
        out_specs=qo,
        out_shape=jax.ShapeDtypeStruct((b, t, d), BF16),
        scratch_shapes=[pltpu.VMEM((2 * TM, t), F32)],
        compiler_params=_params(3),
        name="diffattn",
    )(lam_vec, q, k, v_ext)


def _merge_kernel(x_ref, mod_ref, h_ref, ya_ref, yb_ref, yc_ref, wb_ref, wg_ref, bg_ref, wo_ref, o_ref):
    d = x_ref.shape[-1]
    g = jax.nn.sigmoid(jnp.dot(h_ref[0], wg_ref[...], preferred_element_type=F32) + bg_ref[...])
    m = g[:, :d] * jnp.dot(ya_ref[0], wb_ref[:d, :], preferred_element_type=F32)
    m = m + g[:, d:2 * d] * jnp.dot(yb_ref[0], wb_ref[d:2 * d, :], preferred_element_type=F32)
    m = m + g[:, 2 * d:] * jnp.dot(yc_ref[0], wb_ref[2 * d:, :], preferred_element_type=F32)
    out = jnp.dot(m.astype(BF16), wo_ref[...], preferred_element_type=F32)
    o_ref[0] = x_ref[0] + mod_ref[0, 2:3, :] * out


def _merge(x_all, mod, h, ya, yb, yc, w_branch, w_gate, b_gate, w_out, tile0):
    b, t, d = x_all.shape
    nt = t // TM - tile0
    tok = pl.BlockSpec((1, TM, d), lambda bb, i: (bb, i + tile0, 0))
    return pl.pallas_call(
        _merge_kernel,
        grid=(b, nt),
        in_specs=[
            tok,
            pl.BlockSpec((1, N_MOD, d), lambda bb, i: (jnp.where(i + tile0 == 0, b, bb), 0, 0)),
            tok, tok, tok, tok,
            _const_spec(w_branch.shape), _const_spec(w_gate.shape), _const_spec(b_gate.shape),
            _const_spec(w_out.shape),
        ],
        out_specs=pl.BlockSpec((1, TM, d), lambda bb, i: (bb, i, 0)),
        out_shape=jax.ShapeDtypeStruct((b, nt * TM, d), F32),
        compiler_params=_params(2),
        name="merge",
    )(x_all, mod, h, ya, yb, yc, w_branch, w_gate, b_gate, w_out)


def _ffn_kernel(x_ref, mod_ref, w1_ref, w3_ref, w2_ref, fg_ref, o_ref, *, final_norm):
    x = x_ref[0]
    hb = (_rms(x) * (1.0 + mod_ref[0, 4:5, :]) + mod_ref[0, 3:4, :]).astype(BF16)
    a = jnp.dot(hb, w1_ref[...], preferred_element_type=F32)
    g = jnp.dot(hb, w3_ref[...], preferred_element_type=F32)
    s = (a * jax.nn.sigmoid(a) * g).astype(BF16)
    y = x + mod_ref[0, 5:6, :] * jnp.dot(s, w2_ref[...], preferred_element_type=F32)
    if final_norm:
        y = _rms(y) * fg_ref[...]
    o_ref[0] = y


def _ffn(x_all, mod, w1, w3, w2, final_g, ctx_tiles, final_norm):
    b, t, d = x_all.shape
    tok = pl.BlockSpec((1, TM, d), lambda bb, i: (bb, i, 0))
    return pl.pallas_call(
        functools.partial(_ffn_kernel, final_norm=final_norm),
        grid=(b, t // TM),
        in_specs=[
            tok,
            pl.BlockSpec((1, N_MOD, d), lambda bb, i: (jnp.where(i < ctx_tiles, b, bb), 0, 0)),
            _const_spec(w1.shape), _const_spec(w3.shape), _const_spec(w2.shape),
            _const_spec(final_g.shape),
        ],
        out_specs=tok,
        out_shape=jax.ShapeDtypeStruct((b, t, d), F32),
        compiler_params=_params(2),
        name="ffn",
    )(x_all, mod, w1, w3, w2, final_g)


def _piece_order(n):
    r = jnp.arange(n, dtype=jnp.int32)
    return (r % SUBLANES) * (n // SUBLANES) + r // SUBLANES


def _to_piece_order(v):
    b, n, d = v.shape
    return v.reshape(b, SUBLANES, n // SUBLANES, d).swapaxes(1, 2).reshape(b, n, d)


def _from_piece_order(v):
    b, n, d = v.shape
    return v.reshape(b, n // SUBLANES, SUBLANES, d).swapaxes(1, 2).reshape(b, n, d)


def _rope_tables(n_ctx, n_lat):
    p = _piece_order(n_lat)
    rows = (p // GRID_W).astype(F32)
    cols = (p % GRID_W).astype(F32)
    n_freq = HEAD_DIM // 4
    inv = ROPE_BASE ** (-jnp.arange(n_freq, dtype=F32) / n_freq)
    ang = jnp.concatenate([rows[:, None] * inv, cols[:, None] * inv], axis=-1)
    cos, sin = jnp.cos(ang), jnp.sin(ang)
    reps = LANES // HEAD_DIM
    cos_l = jnp.tile(jnp.concatenate([cos, cos], axis=-1), (1, reps))
    sin_l = jnp.tile(jnp.concatenate([-sin, sin], axis=-1), (1, reps))
    cos_t = jnp.concatenate([jnp.ones((n_ctx, LANES), F32), cos_l], axis=0)
    sin_t = jnp.concatenate([jnp.zeros((n_ctx, LANES), F32), sin_l], axis=0)
    return cos_t, sin_t


def _dft_cos_sin(n, k):
    ang = ((k[:, None] * k[None, :]) % n).astype(F32) * (2.0 * math.pi / n)
    return jnp.cos(ang), jnp.sin(ang)


def _dft_position_matrix(n):
    c, s = _dft_cos_sin(n, _piece_order(n))
    return jnp.concatenate([c, -s], axis=1).astype(BF16)


def kernel(x, c, ctx, c_ctx, ada_w, ada_b, w_in, rnn_conv_w, rnn_conv_b, rnn_wr, rnn_br, rnn_wi, rnn_bi,
           rnn_lambda, attn_lambda, w_branch, w_gate, b_gate, w_out, ffn_w1, ffn_w3, ffn_w2, final_g):
    b, n_lat, d = x.shape
    n_ctx = ctx.shape[1]
    depth = ada_w.shape[0]
    ctx_tiles = n_ctx // TM
    assert n_ctx == TM and n_lat % TM == 0

    cond = jnp.zeros((2 * SUBLANES, d), F32).at[:b].set(c).at[b].set(c_ctx)
    mod_all = _adaln(cond, ada_w, ada_b.reshape(depth, 1, N_MOD * d))

    cos_t, sin_t = _rope_tables(n_ctx, n_lat)
    cc, sc = _dft_cos_sin(FOURIER_GD, jnp.arange(FOURIER_GD, dtype=jnp.int32))
    wf = jnp.concatenate([cc, sc], axis=1).astype(BF16)
    a_lat = _dft_position_matrix(n_lat)
    a_ctx = _dft_position_matrix(n_ctx)
    fg = final_g.reshape(1, d)

    x_all = jnp.concatenate([_to_piece_order(ctx), _to_piece_order(x)], axis=1)
    for l in range(depth):
        last = l == depth - 1
        lam_init = 0.8 - 0.6 * math.exp(-0.3 * l)
        mod = mod_all[l].reshape(2 * SUBLANES, N_MOD, d)
        h, xr, gr, xc, xs, q, k, v_ext = _inproj(x_all, mod, w_in[l].astype(BF16), wf, cos_t, sin_t)
        ya = _rglru(xr, gr, rnn_conv_w[l], rnn_conv_b[l].reshape(1, d), rnn_wr[l].astype(BF16), rnn_br[l],
                    rnn_wi[l].astype(BF16), rnn_bi[l], rnn_lambda[l], n_ctx)
        yb = _fourier(a_lat, a_ctx, xc, xs, n_ctx)
        yc = _attention(attn_lambda[l], q, k, v_ext, n_ctx, lam_init)
        tile0 = ctx_tiles if last else 0
        x_mid = _merge(x_all, mod, h, ya, yb, yc, w_branch[l].astype(BF16), w_gate[l].astype(BF16),
                       b_gate[l].reshape(1, -1), w_out[l].astype(BF16), tile0)
        x_all = _ffn(x_mid, mod, ffn_w1[l].astype(BF16), ffn_w3[l].astype(BF16), ffn_w2[l].astype(BF16),
                     fg, ctx_tiles - tile0, last)
    return _from_piece_order(x_all)
```

```python
import functools
import math

import jax
import jax.numpy as jnp
from jax import lax
from jax.experimental import pallas as pl
from jax.experimental.pallas import tpu as pltpu

F32 = jnp.float32
BF16 = jnp.bfloat16

EPS = 1e-6
N_MOD = 6
GRID_W = 64
RNN_BLOCK = 128
CONV_W = 4
LRU_C = 8.0
FOURIER_GD = 128
N_HEADS = 8
HEAD_DIM = 64
V_DIM = 2 * HEAD_DIM
ROPE_BASE = 10000.0

LANES = 128
SUBLANES = 8
TM = 256
SCAN_UNROLL = 8
KEY_CHUNK = 256
SCORE_LOOKAHEAD = 3
BF16_SUBLANES = 16
VT_ROWS = V_DIM + BF16_SUBLANES
LOG2_E = math.log2(math.e)
VMEM_LIMIT = 56 * 1024 * 1024


def _params(n_axes):
    return pltpu.CompilerParams(
        dimension_semantics=("parallel",) * n_axes, vmem_limit_bytes=VMEM_LIMIT)


def _const_spec(shape):
    nd = len(shape)
    return pl.BlockSpec(shape, lambda *_: (0,) * nd, pipeline_mode=pl.Buffered(1))


def _rms(x):
    return x * lax.rsqrt(jnp.mean(x * x, axis=-1, keepdims=True) + EPS)


def _adaln_kernel(c_ref, w_ref, b_ref, o_ref):
    c = c_ref[...]
    s = (c * jax.nn.sigmoid(c)).astype(BF16)
    o_ref[0] = jnp.dot(s, w_ref[0].astype(BF16), preferred_element_type=F32) + b_ref[0]


def _adaln(cond, ada_w, ada_b):
    depth, d, n = ada_w.shape
    r = cond.shape[0]
    tn = 1536
    return pl.pallas_call(
        _adaln_kernel,
        grid=(depth, n // tn),
        in_specs=[
            pl.BlockSpec((r, d), lambda l, j: (0, 0)),
            pl.BlockSpec((1, d, tn), lambda l, j: (l, 0, j)),
            pl.BlockSpec((1, 1, tn), lambda l, j: (l, 0, j)),
        ],
        out_specs=pl.BlockSpec((1, r, tn), lambda l, j: (l, 0, j)),
        out_shape=jax.ShapeDtypeStruct((depth, r, n), F32),
        compiler_params=_params(2),
        name="adaln",
    )(cond, ada_w, ada_b)


def _inproj_kernel(x_ref, mod_ref, w_ref, wf_ref, cos_ref, sin_ref,
                   h_ref, xr_ref, gr_ref, xc_ref, xs_ref, qt0_ref, qt1_ref, k_ref, vt_ref):
    d = x_ref.shape[-1]
    x = x_ref[0]
    h = _rms(x) * (1.0 + mod_ref[0, 1:2, :]) + mod_ref[0, 0:1, :]
    hb = h.astype(BF16)
    h_ref[0] = hb

    def proj(j):
        return jnp.dot(hb, w_ref[:, j * d:(j + 1) * d], preferred_element_type=F32)

    xr_ref[0] = proj(0)
    gr_ref[0] = proj(1)

    xf = proj(2).astype(BF16)
    for g in range(d // FOURIER_GD):
        sl = slice(g * FOURIER_GD, (g + 1) * FOURIER_GD)
        z = jnp.dot(xf[:, sl], wf_ref[...], preferred_element_type=F32)
        xc_ref[0, :, sl] = z[:, :FOURIER_GD].astype(BF16)
        xs_ref[0, :, sl] = z[:, FOURIER_GD:].astype(BF16)

    cos = cos_ref[...]
    sin = sin_ref[...]
    lane = lax.broadcasted_iota(jnp.int32, cos.shape, 1)
    first_half = (lane % HEAD_DIM) < (HEAD_DIM // 2)

    def rope(th):
        swapped = jnp.where(first_half,
                            pltpu.roll(th, LANES - HEAD_DIM // 2, 1),
                            pltpu.roll(th, HEAD_DIM // 2, 1))
        return th * cos + swapped * sin

    q = proj(3)
    row = lax.broadcasted_iota(jnp.int32, (V_DIM, q.shape[0]), 0)
    zero_t = jnp.zeros((V_DIM, q.shape[0]), F32)
    for hh in range(d // V_DIM):
        sl = slice(hh * V_DIM, (hh + 1) * V_DIM)
        qt = (rope(q[:, sl]) * (HEAD_DIM ** -0.5 * LOG2_E)).T
        qt0_ref[0, sl, :] = jnp.where(row < HEAD_DIM, qt, zero_t).astype(BF16)
        qt1_ref[0, sl, :] = jnp.where(row >= HEAD_DIM, qt, zero_t).astype(BF16)

    k = proj(4)
    for hh in range(d // LANES):
        sl = slice(hh * LANES, (hh + 1) * LANES)
        k_ref[0, :, sl] = rope(k[:, sl]).astype(BF16)

    v = proj(5)
    ones = jnp.ones((VT_ROWS - V_DIM, v.shape[0]), BF16)
    for hh in range(d // V_DIM):
        vt_ref[0, hh * VT_ROWS:hh * VT_ROWS + V_DIM, :] = v[:, hh * V_DIM:(hh + 1) * V_DIM].T.astype(BF16)
        vt_ref[0, hh * VT_ROWS + V_DIM:(hh + 1) * VT_ROWS, :] = ones


def _inproj(x_all, mod, w_in, wf, cos_t, sin_t):
    b, t, d = x_all.shape
    nt = t // TM
    vt_rows = d // V_DIM * VT_ROWS
    tok_t = lambda rows: pl.BlockSpec((1, rows, TM), lambda bb, i: (bb, 0, i))
    tok = pl.BlockSpec((1, TM, d), lambda bb, i: (bb, i, 0))
    f32_out = jax.ShapeDtypeStruct((b, t, d), F32)
    bf_out = jax.ShapeDtypeStruct((b, t, d), BF16)
    return pl.pallas_call(
        _inproj_kernel,
        grid=(b, nt),
        in_specs=[
            tok,
            pl.BlockSpec((1, N_MOD, d), lambda bb, i: (jnp.where(i == 0, b, bb), 0, 0)),
            _const_spec(w_in.shape),
            _const_spec(wf.shape),
            pl.BlockSpec((TM, LANES), lambda bb, i: (i, 0)),
            pl.BlockSpec((TM, LANES), lambda bb, i: (i, 0)),
        ],
        out_specs=[tok] * 5 + [tok_t(d), tok_t(d), tok, tok_t(vt_rows)],
        out_shape=[bf_out, f32_out, f32_out, bf_out, bf_out,
                   jax.ShapeDtypeStruct((b, d, t), BF16), jax.ShapeDtypeStruct((b, d, t), BF16), bf_out,
                   jax.ShapeDtypeStruct((b, vt_rows, t), BF16)],
        compiler_params=_params(2),
        name="inproj",
    )(x_all, mod, w_in, wf, cos_t, sin_t)


def _log_sigmoid(x):
    return jnp.minimum(x, 0.0) - jnp.log(1.0 + jnp.exp(-jnp.abs(x)))


def _sigmoid(x):
    return 0.5 * jnp.tanh(0.5 * x) + 0.5


def _rglru_kernel(xr_ref, gr_ref, cw_ref, cb_ref, wr_ref, br_ref, wi_ref, bi_ref, lam_ref, ya_ref,
                  xpad, a0, b0, a1, b1, h0, h1, *, n_ctx):
    t_all, c = xr_ref.shape[1], xr_ref.shape[2]
    n_piece = SUBLANES
    w = [cw_ref[k:k + 1, :] for k in range(CONV_W)]
    sub = lax.broadcasted_iota(jnp.int32, (n_piece, c), 0)
    zeros = jnp.zeros((n_piece, c), F32)
    ones = jnp.ones((n_piece, c), F32)

    def from_prev_piece(v):
        return jnp.where(sub >= 1, pltpu.roll(v, 1, 0), 0.0)

    def from_next_piece(v):
        return jnp.where(sub < n_piece - 1, pltpu.roll(v, n_piece - 1, 0), 0.0)

    def run_sequence(base, n, state_f, state_r):
        steps = n // n_piece
        x3 = xr_ref[0, base:base + n, :].reshape(steps, n_piece, c)
        xpad[2:steps + 2] = x3
        xpad[0] = from_prev_piece(x3[steps - 2])
        xpad[1] = from_prev_piece(x3[steps - 1])
        xpad[steps + 2] = from_next_piece(x3[0])
        u = xpad[0:steps] * w[0]
        u = u + xpad[1:steps + 1] * w[1]
        u = u + xpad[2:steps + 2] * w[2]
        u = u + xpad[3:steps + 3] * w[3]
        u = (u + cb_ref[...]).reshape(n, c)
        ub = u.astype(BF16)
        for dd, (a_s, b_s) in enumerate(((a0, b0), (a1, b1))):
            r = _sigmoid(jnp.dot(ub, wr_ref[dd, 0], preferred_element_type=F32) + br_ref[dd:dd + 1, :])
            gate_i = _sigmoid(jnp.dot(ub, wi_ref[dd, 0], preferred_element_type=F32) + bi_ref[dd:dd + 1, :])
            a = jnp.exp2(r * ((LRU_C * LOG2_E) * _log_sigmoid(lam_ref[dd:dd + 1, :])))
            a_s[0:steps] = a.reshape(steps, n_piece, c)
            b_s[0:steps] = (jnp.sqrt(1.0 - a * a) * gate_i * u).reshape(steps, n_piece, c)

        def piece_finals(j, carry):
            hf, pf, hr, pr = carry
            af = a0[j]
            ar = a1[steps - 1 - j]
            return af * hf + b0[j], af * pf, ar * hr + b1[steps - 1 - j], ar * pr

        hf, pf, hr, pr = lax.fori_loop(0, steps, piece_finals, (zeros, ones, zeros, ones), unroll=SCAN_UNROLL)

        init_f = zeros
        for s in range(n_piece):
            init_f = jnp.where(sub == s, state_f, init_f)
            state_f = hf[s:s + 1, :] + pf[s:s + 1, :] * state_f
        init_r = zeros
        for s in reversed(range(n_piece)):
            init_r = jnp.where(sub == s, state_r, init_r)
            state_r = hr[s:s + 1, :] + pr[s:s + 1, :] * state_r

        def piece_states(j, carry):
            hf, hr = carry
            hf = a0[j] * hf + b0[j]
            h0[j] = hf
            hr = a1[steps - 1 - j] * hr + b1[steps - 1 - j]
            h1[steps - 1 - j] = hr
            return hf, hr

        lax.fori_loop(0, steps, piece_states, (init_f, init_r), unroll=SCAN_UNROLL)
        h = (h0[0:steps] + h1[0:steps]).reshape(n, c)
        ya_ref[0, base:base + n, :] = (h * jax.nn.gelu(gr_ref[0, base:base + n, :])).astype(BF16)
        return state_f, state_r

    zero_state = jnp.zeros((1, c), F32)
    state_f, state_r = run_sequence(0, n_ctx, zero_state, zero_state)
    run_sequence(n_ctx, t_all - n_ctx, state_f, state_r)


def _rglru(xr, gr, conv_w, conv_b, wr, br, wi, bi, lam, n_ctx):
    b, t, d = xr.shape
    c = RNN_BLOCK
    steps = max(n_ctx, t - n_ctx) // SUBLANES
    seq = pl.BlockSpec((1, t, c), lambda bb, g: (bb, 0, g))
    vec = lambda rows: pl.BlockSpec((rows, c), lambda bb, g: (0, g))
    mat = pl.BlockSpec((2, 1, c, c), lambda bb, g: (0, g, 0, 0))
    return pl.pallas_call(
        functools.partial(_rglru_kernel, n_ctx=n_ctx),
        grid=(b, d // c),
        in_specs=[seq, seq, vec(CONV_W), vec(1), mat, vec(2), mat, vec(2), vec(2)],
        out_specs=seq,
        out_shape=jax.ShapeDtypeStruct((b, t, d), BF16),
        scratch_shapes=([pltpu.VMEM((steps + CONV_W - 1, SUBLANES, c), F32)]
                        + [pltpu.VMEM((steps, SUBLANES, c), F32)] * 6),
        compiler_params=_params(2),
        name="rglru",
    )(xr, gr, conv_w, conv_b, wr, br, wi, bi, lam)


def _fourier_kernel(al_ref, ac_ref, xc_ref, xs_ref, y_ref, *, n_ctx, scale_ctx, scale_lat):
    i = pl.program_id(1)
    n_lat = xc_ref.shape[1] - n_ctx

    def mix(a_ref, lo, n, scale):
        y = jnp.dot(a_ref[:, :n], xc_ref[0, lo:lo + n, :], preferred_element_type=F32)
        y = y + jnp.dot(a_ref[:, n:], xs_ref[0, lo:lo + n, :], preferred_element_type=F32)
        y_ref[0] = (y * scale).astype(BF16)

    @pl.when(i == 0)
    def _():
        mix(ac_ref, 0, n_ctx, scale_ctx)

    @pl.when(i > 0)
    def _():
        mix(al_ref, n_ctx, n_lat, scale_lat)


def _fourier(a_lat, a_ctx, xc, xs, n_ctx):
    b, t, d = xc.shape
    n_lat = t - n_ctx
    assert n_ctx == TM
    slab = pl.BlockSpec((1, t, d), lambda bb, i: (bb, 0, 0))
    return pl.pallas_call(
        functools.partial(_fourier_kernel, n_ctx=n_ctx,
                          scale_ctx=(n_ctx * FOURIER_GD) ** -0.5,
                          scale_lat=(n_lat * FOURIER_GD) ** -0.5),
        grid=(b, t // TM),
        in_specs=[
            pl.BlockSpec((TM, 2 * n_lat), lambda bb, i: (jnp.maximum(i - 1, 0), 0)),
            _const_spec(a_ctx.shape),
            slab, slab,
        ],
        out_specs=pl.BlockSpec((1, TM, d), lambda bb, i: (bb, i, 0)),
        out_shape=jax.ShapeDtypeStruct((b, t, d), BF16),
        compiler_params=_params(2),
        name="fourier",
    )(a_lat, a_ctx, xc, xs)


def _attn_lambda_kernel(lv_ref, o_ref, *, lam_inits):
    for l, lam_init in enumerate(lam_inits):
        lv = lv_ref[l]
        lam = (jnp.exp(jnp.sum(lv[0:1] * lv[1:2], axis=1, keepdims=True))
               - jnp.exp(jnp.sum(lv[2:3] * lv[3:4], axis=1, keepdims=True)) + lam_init)
        o_ref[l] = jnp.broadcast_to(lam, o_ref.shape[1:])


def _attn_lambda(attn_lambda, lam_inits):
    depth = attn_lambda.shape[0]
    return pl.pallas_call(
        functools.partial(_attn_lambda_kernel, lam_inits=lam_inits),
        out_shape=jax.ShapeDtypeStruct((depth, 1, TM), F32),
        name="attn_lambda",
    )(attn_lambda)


def _attn_kernel(lam_ref, qt0_ref, qt1_ref, k_ref, vt_ref, o_ref, *, n_ctx, lam_init):
    i = pl.program_id(2)
    qts = (qt0_ref[0], qt1_ref[0])

    def attend(n_keys):
        m = [None, None]
        u = [None, None]

        def scores(c0):
            kc = k_ref[0, c0:c0 + KEY_CHUNK, :]
            return [jnp.dot(kc, qts[hf], preferred_element_type=F32) for hf in range(2)]

        starts = list(range(0, n_keys, KEY_CHUNK))
        pending = [scores(c0) for c0 in starts[:SCORE_LOOKAHEAD]]
        for n, c0 in enumerate(starts):
            if n + SCORE_LOOKAHEAD < len(starts):
                pending.append(scores(starts[n + SCORE_LOOKAHEAD]))
            sts = pending.pop(0)
            vtc = vt_ref[0, :, c0:c0 + KEY_CHUNK]
            for hf in range(2):
                m_chunk = jnp.max(sts[hf], axis=0, keepdims=True)
                m_new = m_chunk if m[hf] is None else jnp.maximum(m[hf], m_chunk)
                pt = jnp.exp2(sts[hf] - m_new).astype(BF16)
                part = jnp.dot(vtc, pt, preferred_element_type=F32)
                u[hf] = part if u[hf] is None else u[hf] * jnp.exp2(m[hf] - m_new) + part
                m[hf] = m_new
        ot = (u[0][:V_DIM] * (1.0 / u[0][V_DIM:V_DIM + 1])
              - u[1][:V_DIM] * (lam_ref[...] / u[1][V_DIM:V_DIM + 1]))
        ot = ot * lax.rsqrt(jnp.mean(ot * ot, axis=0, keepdims=True) + EPS) * (1.0 - lam_init)
        o_ref[0] = ot.T.astype(BF16)

    @pl.when(i == 0)
    def _():
        attend(n_ctx)

    @pl.when(i > 0)
    def _():
        attend(k_ref.shape[1])


def _attention(lam_row, qt0, qt1, k, vt, n_ctx, lam_init):
    b, t, d = k.shape
    assert n_ctx == TM
    qt = pl.BlockSpec((1, V_DIM, TM), lambda bb, hh, i: (bb, hh, i))
    keys = pl.BlockSpec((1, t, V_DIM), lambda bb, hh, i: (bb, 0, hh))
    vals = pl.BlockSpec((1, VT_ROWS, t), lambda bb, hh, i: (bb, hh, 0))
    return pl.pallas_call(
        functools.partial(_attn_kernel, n_ctx=n_ctx, lam_init=lam_init),
        grid=(b, d // V_DIM, t // TM),
        in_specs=[pl.BlockSpec(lam_row.shape, lambda bb, hh, i: (0, 0)), qt, qt, keys, vals],
        out_specs=pl.BlockSpec((1, TM, V_DIM), lambda bb, hh, i: (bb, i, hh)),
        out_shape=jax.ShapeDtypeStruct((b, t, d), BF16),
        compiler_params=_params(3),
        name="diffattn",
    )(lam_row, qt0, qt1, k, vt)


def _merge_kernel(x_ref, mod_ref, h_ref, ya_ref, yb_ref, yc_ref, wb_ref, wg_ref, bg_ref, wo_ref, o_ref):
    d = x_ref.shape[-1]
    g = jax.nn.sigmoid(jnp.dot(h_ref[0], wg_ref[...], preferred_element_type=F32) + bg_ref[...])
    m = g[:, :d] * jnp.dot(ya_ref[0], wb_ref[:d, :], preferred_element_type=F32)
    m = m + g[:, d:2 * d] * jnp.dot(yb_ref[0], wb_ref[d:2 * d, :], preferred_element_type=F32)
    m = m + g[:, 2 * d:] * jnp.dot(yc_ref[0], wb_ref[2 * d:, :], preferred_element_type=F32)
    out = jnp.dot(m.astype(BF16), wo_ref[...], preferred_element_type=F32)
    o_ref[0] = x_ref[0] + mod_ref[0, 2:3, :] * out


def _merge(x_all, mod, h, ya, yb, yc, w_branch, w_gate, b_gate, w_out, tile0):
    b, t, d = x_all.shape
    nt = t // TM - tile0
    tok = pl.BlockSpec((1, TM, d), lambda bb, i: (bb, i + tile0, 0))
    return pl.pallas_call(
        _merge_kernel,
        grid=(b, nt),
        in_specs=[
            tok,
            pl.BlockSpec((1, N_MOD, d), lambda bb, i: (jnp.where(i + tile0 == 0, b, bb), 0, 0)),
            tok, tok, tok, tok,
            _const_spec(w_branch.shape), _const_spec(w_gate.shape), _const_spec(b_gate.shape),
            _const_spec(w_out.shape),
        ],
        out_specs=pl.BlockSpec((1, TM, d), lambda bb, i: (bb, i, 0)),
        out_shape=jax.ShapeDtypeStruct((b, nt * TM, d), F32),
        compiler_params=_params(2),
        name="merge",
    )(x_all, mod, h, ya, yb, yc, w_branch, w_gate, b_gate, w_out)


def _ffn_kernel(x_ref, mod_ref, w1_ref, w3_ref, w2_ref, fg_ref, o_ref, *, final_norm):
    x = x_ref[0]
    hb = (_rms(x) * (1.0 + mod_ref[0, 4:5, :]) + mod_ref[0, 3:4, :]).astype(BF16)
    a = jnp.dot(hb, w1_ref[...], preferred_element_type=F32)
    g = jnp.dot(hb, w3_ref[...], preferred_element_type=F32)
    s = (a * jax.nn.sigmoid(a) * g).astype(BF16)
    y = x + mod_ref[0, 5:6, :] * jnp.dot(s, w2_ref[...], preferred_element_type=F32)
    if final_norm:
        y = _rms(y) * fg_ref[...]
    o_ref[0] = y


def _ffn(x_all, mod, w1, w3, w2, final_g, ctx_tiles, final_norm):
    b, t, d = x_all.shape
    tok = pl.BlockSpec((1, TM, d), lambda bb, i: (bb, i, 0))
    return pl.pallas_call(
        functools.partial(_ffn_kernel, final_norm=final_norm),
        grid=(b, t // TM),
        in_specs=[
            tok,
            pl.BlockSpec((1, N_MOD, d), lambda bb, i: (jnp.where(i < ctx_tiles, b, bb), 0, 0)),
            _const_spec(w1.shape), _const_spec(w3.shape), _const_spec(w2.shape),
            _const_spec(final_g.shape),
        ],
        out_specs=tok,
        out_shape=jax.ShapeDtypeStruct((b, t, d), F32),
        compiler_params=_params(2),
        name="ffn",
    )(x_all, mod, w1, w3, w2, final_g)


def _piece_order(n):
    r = jnp.arange(n, dtype=jnp.int32)
    return (r % SUBLANES) * (n // SUBLANES) + r // SUBLANES


def _to_piece_order(v):
    b, n, d = v.shape
    return v.reshape(b, SUBLANES, n // SUBLANES, d).swapaxes(1, 2).reshape(b, n, d)


def _from_piece_order(v):
    b, n, d = v.shape
    return v.reshape(b, n // SUBLANES, SUBLANES, d).swapaxes(1, 2).reshape(b, n, d)


def _rope_tables(n_ctx, n_lat):
    p = _piece_order(n_lat)
    rows = (p // GRID_W).astype(F32)
    cols = (p % GRID_W).astype(F32)
    n_freq = HEAD_DIM // 4
    inv = ROPE_BASE ** (-jnp.arange(n_freq, dtype=F32) / n_freq)
    ang = jnp.concatenate([rows[:, None] * inv, cols[:, None] * inv], axis=-1)
    cos, sin = jnp.cos(ang), jnp.sin(ang)
    reps = LANES // HEAD_DIM
    cos_l = jnp.tile(jnp.concatenate([cos, cos], axis=-1), (1, reps))
    sin_l = jnp.tile(jnp.concatenate([-sin, sin], axis=-1), (1, reps))
    cos_t = jnp.concatenate([jnp.ones((n_ctx, LANES), F32), cos_l], axis=0)
    sin_t = jnp.concatenate([jnp.zeros((n_ctx, LANES), F32), sin_l], axis=0)
    return cos_t, sin_t


def _dft_cos_sin(n, k):
    ang = ((k[:, None] * k[None, :]) % n).astype(F32) * (2.0 * math.pi / n)
    return jnp.cos(ang), jnp.sin(ang)


def _dft_position_matrix(n):
    c, s = _dft_cos_sin(n, _piece_order(n))
    return jnp.concatenate([c, -s], axis=1).astype(BF16)


def kernel(x, c, ctx, c_ctx, ada_w, ada_b, w_in, rnn_conv_w, rnn_conv_b, rnn_wr, rnn_br, rnn_wi, rnn_bi,
           rnn_lambda, attn_lambda, w_branch, w_gate, b_gate, w_out, ffn_w1, ffn_w3, ffn_w2, final_g):
    b, n_lat, d = x.shape
    n_ctx = ctx.shape[1]
    depth = ada_w.shape[0]
    ctx_tiles = n_ctx // TM
    assert n_ctx == TM and n_lat % TM == 0

    cond = jnp.zeros((2 * SUBLANES, d), F32).at[:b].set(c).at[b].set(c_ctx)
    mod_all = _adaln(cond, ada_w, ada_b.reshape(depth, 1, N_MOD * d))

    cos_t, sin_t = _rope_tables(n_ctx, n_lat)
    cc, sc = _dft_cos_sin(FOURIER_GD, jnp.arange(FOURIER_GD, dtype=jnp.int32))
    wf = jnp.concatenate([cc, sc], axis=1).astype(BF16)
    a_lat = _dft_position_matrix(n_lat)
    a_ctx = _dft_position_matrix(n_ctx)
    fg = final_g.reshape(1, d)

    x_all = jnp.concatenate([_to_piece_order(ctx), _to_piece_order(x)], axis=1)
    lam_inits = tuple(0.8 - 0.6 * math.exp(-0.3 * l) for l in range(depth))
    lam_rows = _attn_lambda(attn_lambda, lam_inits)
    for l in range(depth):
        last = l == depth - 1
        mod = mod_all[l].reshape(2 * SUBLANES, N_MOD, d)
        h, xr, gr, xc, xs, qt0, qt1, k, vt = _inproj(x_all, mod, w_in[l].astype(BF16), wf, cos_t, sin_t)
        ya = _rglru(xr, gr, rnn_conv_w[l], rnn_conv_b[l].reshape(1, d), rnn_wr[l].astype(BF16), rnn_br[l],
                    rnn_wi[l].astype(BF16), rnn_bi[l], rnn_lambda[l], n_ctx)
        yb = _fourier(a_lat, a_ctx, xc, xs, n_ctx)
        yc = _attention(lam_rows[l], qt0, qt1, k, vt, n_ctx, lam_inits[l])
        tile0 = ctx_tiles if last else 0
        x_mid = _merge(x_all, mod, h, ya, yb, yc, w_branch[l].astype(BF16), w_gate[l].astype(BF16),
                       b_gate[l].reshape(1, -1), w_out[l].astype(BF16), tile0)
        x_all = _ffn(x_mid, mod, ffn_w1[l].astype(BF16), ffn_w3[l].astype(BF16), ffn_w2[l].astype(BF16),
                     fg, ctx_tiles - tile0, last)
    return _from_piece_order(x_all)
```

```python
import functools
import math

import jax
import jax.numpy as jnp
import numpy as np
from jax import lax
from jax.experimental import pallas as pl
from jax.experimental.pallas import tpu as pltpu

F32 = jnp.float32
BF16 = jnp.bfloat16

EPS = 1e-6
N_MOD = 6
GRID_W = 64
RNN_BLOCK = 128
CONV_W = 4
LRU_C = 8.0
FOURIER_GD = 128
N_HEADS = 8
HEAD_DIM = 64
V_DIM = 2 * HEAD_DIM
ROPE_BASE = 10000.0

LANES = 128
SUBLANES = 8
TM = 256
SCAN_UNROLL = 8
KEY_CHUNK = 256
SCORE_LOOKAHEAD = 3
HEADS_PER_STEP = 4
BF16_SUBLANES = 16
VT_ROWS = V_DIM + BF16_SUBLANES
LOG2_E = math.log2(math.e)
VMEM_LIMIT = 56 * 1024 * 1024


def _params(n_axes):
    return pltpu.CompilerParams(
        dimension_semantics=("parallel",) * n_axes, vmem_limit_bytes=VMEM_LIMIT)


def _const_spec(shape):
    nd = len(shape)
    return pl.BlockSpec(shape, lambda *_: (0,) * nd, pipeline_mode=pl.Buffered(1))


def _rms(x):
    return x * lax.rsqrt(jnp.mean(x * x, axis=-1, keepdims=True) + EPS)


def _adaln_kernel(c_ref, w_ref, b_ref, o_ref):
    c = c_ref[...]
    s = (c * jax.nn.sigmoid(c)).astype(BF16)
    o_ref[0] = jnp.dot(s, w_ref[0].astype(BF16), preferred_element_type=F32) + b_ref[0]


def _adaln(cond, ada_w, ada_b):
    depth, d, n = ada_w.shape
    r = cond.shape[0]
    tn = 1536
    return pl.pallas_call(
        _adaln_kernel,
        grid=(depth, n // tn),
        in_specs=[
            pl.BlockSpec((r, d), lambda l, j: (0, 0)),
            pl.BlockSpec((1, d, tn), lambda l, j: (l, 0, j)),
            pl.BlockSpec((1, 1, tn), lambda l, j: (l, 0, j)),
        ],
        out_specs=pl.BlockSpec((1, r, tn), lambda l, j: (l, 0, j)),
        out_shape=jax.ShapeDtypeStruct((depth, r, n), F32),
        compiler_params=_params(2),
        name="adaln",
    )(cond, ada_w, ada_b)


def _inproj_kernel(x_ref, mod_ref, w_ref, wf_ref, cos_ref, sin_ref,
                   h_ref, xr_ref, gr_ref, xc_ref, xs_ref, qt0_ref, qt1_ref, k_ref, vt_ref):
    d = x_ref.shape[-1]
    x = x_ref[0]
    h = _rms(x) * (1.0 + mod_ref[0, 1:2, :]) + mod_ref[0, 0:1, :]
    hb = h.astype(BF16)
    h_ref[0] = hb

    def proj(j):
        return jnp.dot(hb, w_ref[:, j * d:(j + 1) * d], preferred_element_type=F32)

    xr_ref[0] = proj(0)
    gr_ref[0] = proj(1)

    xf = proj(2).astype(BF16)
    for g in range(d // FOURIER_GD):
        sl = slice(g * FOURIER_GD, (g + 1) * FOURIER_GD)
        z = jnp.dot(xf[:, sl], wf_ref[...], preferred_element_type=F32)
        xc_ref[0, :, sl] = z[:, :FOURIER_GD].astype(BF16)
        xs_ref[0, :, sl] = z[:, FOURIER_GD:].astype(BF16)

    cos = cos_ref[...]
    sin = sin_ref[...]
    lane = lax.broadcasted_iota(jnp.int32, cos.shape, 1)
    first_half = (lane % HEAD_DIM) < (HEAD_DIM // 2)

    def rope(th):
        swapped = jnp.where(first_half,
                            pltpu.roll(th, LANES - HEAD_DIM // 2, 1),
                            pltpu.roll(th, HEAD_DIM // 2, 1))
        return th * cos + swapped * sin

    q = proj(3)
    row = lax.broadcasted_iota(jnp.int32, (V_DIM, q.shape[0]), 0)
    zero_t = jnp.zeros((V_DIM, q.shape[0]), F32)
    for hh in range(d // V_DIM):
        sl = slice(hh * V_DIM, (hh + 1) * V_DIM)
        qt = (rope(q[:, sl]) * (HEAD_DIM ** -0.5 * LOG2_E)).T
        qt0_ref[0, sl, :] = jnp.where(row < HEAD_DIM, qt, zero_t).astype(BF16)
        qt1_ref[0, sl, :] = jnp.where(row >= HEAD_DIM, qt, zero_t).astype(BF16)

    k = proj(4)
    for hh in range(d // LANES):
        sl = slice(hh * LANES, (hh + 1) * LANES)
        k_ref[0, :, sl] = rope(k[:, sl]).astype(BF16)

    v = proj(5)
    ones = jnp.ones((VT_ROWS - V_DIM, v.shape[0]), BF16)
    for hh in range(d // V_DIM):
        vt_ref[0, hh * VT_ROWS:hh * VT_ROWS + V_DIM, :] = v[:, hh * V_DIM:(hh + 1) * V_DIM].T.astype(BF16)
        vt_ref[0, hh * VT_ROWS + V_DIM:(hh + 1) * VT_ROWS, :] = ones


def _inproj(x_all, mod, w_in, wf, cos_t, sin_t):
    b, t, d = x_all.shape
    nt = t // TM
    vt_rows = d // V_DIM * VT_ROWS
    tok_t = lambda rows: pl.BlockSpec((1, rows, TM), lambda bb, i: (bb, 0, i))
    tok = pl.BlockSpec((1, TM, d), lambda bb, i: (bb, i, 0))
    f32_out = jax.ShapeDtypeStruct((b, t, d), F32)
    bf_out = jax.ShapeDtypeStruct((b, t, d), BF16)
    return pl.pallas_call(
        _inproj_kernel,
        grid=(b, nt),
        in_specs=[
            tok,
            pl.BlockSpec((1, N_MOD, d), lambda bb, i: (jnp.where(i == 0, b, bb), 0, 0)),
            _const_spec(w_in.shape),
            _const_spec(wf.shape),
            pl.BlockSpec((TM, LANES), lambda bb, i: (i, 0)),
            pl.BlockSpec((TM, LANES), lambda bb, i: (i, 0)),
        ],
        out_specs=[tok] * 5 + [tok_t(d), tok_t(d), tok, tok_t(vt_rows)],
        out_shape=[bf_out, f32_out, f32_out, bf_out, bf_out,
                   jax.ShapeDtypeStruct((b, d, t), BF16), jax.ShapeDtypeStruct((b, d, t), BF16), bf_out,
                   jax.ShapeDtypeStruct((b, vt_rows, t), BF16)],
        compiler_params=_params(2),
        name="inproj",
    )(x_all, mod, w_in, wf, cos_t, sin_t)


def _log_sigmoid(x):
    return jnp.minimum(x, 0.0) - jnp.log(1.0 + jnp.exp(-jnp.abs(x)))


def _sigmoid(x):
    return 0.5 * jnp.tanh(0.5 * x) + 0.5


def _rglru_kernel(xr_ref, gr_ref, cw_ref, cb_ref, wr_ref, br_ref, wi_ref, bi_ref, lam_ref, ya_ref,
                  xpad, a0, b0, a1, b1, h0, h1, *, n_ctx):
    t_all, c = xr_ref.shape[1], xr_ref.shape[2]
    n_piece = SUBLANES
    w = [cw_ref[k:k + 1, :] for k in range(CONV_W)]
    sub = lax.broadcasted_iota(jnp.int32, (n_piece, c), 0)
    zeros = jnp.zeros((n_piece, c), F32)
    ones = jnp.ones((n_piece, c), F32)

    def from_prev_piece(v):
        return jnp.where(sub >= 1, pltpu.roll(v, 1, 0), 0.0)

    def from_next_piece(v):
        return jnp.where(sub < n_piece - 1, pltpu.roll(v, n_piece - 1, 0), 0.0)

    def run_sequence(base, n, state_f, state_r):
        steps = n // n_piece
        x3 = xr_ref[0, base:base + n, :].reshape(steps, n_piece, c)
        xpad[2:steps + 2] = x3
        xpad[0] = from_prev_piece(x3[steps - 2])
        xpad[1] = from_prev_piece(x3[steps - 1])
        xpad[steps + 2] = from_next_piece(x3[0])
        u = xpad[0:steps] * w[0]
        u = u + xpad[1:steps + 1] * w[1]
        u = u + xpad[2:steps + 2] * w[2]
        u = u + xpad[3:steps + 3] * w[3]
        u = (u + cb_ref[...]).reshape(n, c)
        ub = u.astype(BF16)
        for dd, (a_s, b_s) in enumerate(((a0, b0), (a1, b1))):
            r = _sigmoid(jnp.dot(ub, wr_ref[dd, 0], preferred_element_type=F32) + br_ref[dd:dd + 1, :])
            gate_i = _sigmoid(jnp.dot(ub, wi_ref[dd, 0], preferred_element_type=F32) + bi_ref[dd:dd + 1, :])
            a = jnp.exp2(r * ((LRU_C * LOG2_E) * _log_sigmoid(lam_ref[dd:dd + 1, :])))
            a_s[0:steps] = a.reshape(steps, n_piece, c)
            b_s[0:steps] = (jnp.sqrt(1.0 - a * a) * gate_i * u).reshape(steps, n_piece, c)

        def piece_finals(j, carry):
            hf, pf, hr, pr = carry
            af = a0[j]
            ar = a1[steps - 1 - j]
            return af * hf + b0[j], af * pf, ar * hr + b1[steps - 1 - j], ar * pr

        hf, pf, hr, pr = lax.fori_loop(0, steps, piece_finals, (zeros, ones, zeros, ones), unroll=SCAN_UNROLL)

        init_f = zeros
        for s in range(n_piece):
            init_f = jnp.where(sub == s, state_f, init_f)
            state_f = hf[s:s + 1, :] + pf[s:s + 1, :] * state_f
        init_r = zeros
        for s in reversed(range(n_piece)):
            init_r = jnp.where(sub == s, state_r, init_r)
            state_r = hr[s:s + 1, :] + pr[s:s + 1, :] * state_r

        def piece_states(j, carry):
            hf, hr = carry
            hf = a0[j] * hf + b0[j]
            h0[j] = hf
            hr = a1[steps - 1 - j] * hr + b1[steps - 1 - j]
            h1[steps - 1 - j] = hr
            return hf, hr

        lax.fori_loop(0, steps, piece_states, (init_f, init_r), unroll=SCAN_UNROLL)
        h = (h0[0:steps] + h1[0:steps]).reshape(n, c)
        ya_ref[0, base:base + n, :] = (h * jax.nn.gelu(gr_ref[0, base:base + n, :])).astype(BF16)
        return state_f, state_r

    zero_state = jnp.zeros((1, c), F32)
    state_f, state_r = run_sequence(0, n_ctx, zero_state, zero_state)
    run_sequence(n_ctx, t_all - n_ctx, state_f, state_r)


def _rglru(xr, gr, conv_w, conv_b, wr, br, wi, bi, lam, n_ctx):
    b, t, d = xr.shape
    c = RNN_BLOCK
    steps = max(n_ctx, t - n_ctx) // SUBLANES
    seq = pl.BlockSpec((1, t, c), lambda bb, g: (bb, 0, g))
    vec = lambda rows: pl.BlockSpec((rows, c), lambda bb, g: (0, g))
    mat = pl.BlockSpec((2, 1, c, c), lambda bb, g: (0, g, 0, 0))
    return pl.pallas_call(
        functools.partial(_rglru_kernel, n_ctx=n_ctx),
        grid=(b, d // c),
        in_specs=[seq, seq, vec(CONV_W), vec(1), mat, vec(2), mat, vec(2), vec(2)],
        out_specs=seq,
        out_shape=jax.ShapeDtypeStruct((b, t, d), BF16),
        scratch_shapes=([pltpu.VMEM((steps + CONV_W - 1, SUBLANES, c), F32)]
                        + [pltpu.VMEM((steps, SUBLANES, c), F32)] * 6),
        compiler_params=_params(2),
        name="rglru",
    )(xr, gr, conv_w, conv_b, wr, br, wi, bi, lam)


def _fourier_kernel(al_ref, ac_ref, xc_ref, xs_ref, y_ref, *, n_ctx, scale_ctx, scale_lat):
    i = pl.program_id(1)
    n_lat = xc_ref.shape[1] - n_ctx

    def mix(a_ref, lo, n, scale):
        y = jnp.dot(a_ref[:, :n], xc_ref[0, lo:lo + n, :], preferred_element_type=F32)
        y = y + jnp.dot(a_ref[:, n:], xs_ref[0, lo:lo + n, :], preferred_element_type=F32)
        y_ref[0] = (y * scale).astype(BF16)

    @pl.when(i == 0)
    def _():
        mix(ac_ref, 0, n_ctx, scale_ctx)

    @pl.when(i > 0)
    def _():
        mix(al_ref, n_ctx, n_lat, scale_lat)


def _fourier(a_lat, a_ctx, xc, xs, n_ctx):
    b, t, d = xc.shape
    n_lat = t - n_ctx
    assert n_ctx == TM
    slab = pl.BlockSpec((1, t, d), lambda bb, i: (bb, 0, 0))
    return pl.pallas_call(
        functools.partial(_fourier_kernel, n_ctx=n_ctx,
                          scale_ctx=(n_ctx * FOURIER_GD) ** -0.5,
                          scale_lat=(n_lat * FOURIER_GD) ** -0.5),
        grid=(b, t // TM),
        in_specs=[
            pl.BlockSpec((TM, 2 * n_lat), lambda bb, i: (jnp.maximum(i - 1, 0), 0)),
            _const_spec(a_ctx.shape),
            slab, slab,
        ],
        out_specs=pl.BlockSpec((1, TM, d), lambda bb, i: (bb, i, 0)),
        out_shape=jax.ShapeDtypeStruct((b, t, d), BF16),
        compiler_params=_params(2),
        name="fourier",
    )(a_lat, a_ctx, xc, xs)


def _attn_lambda_kernel(lv_ref, o_ref, *, lam_inits):
    for l, lam_init in enumerate(lam_inits):
        lv = lv_ref[l]
        lam = (jnp.exp(jnp.sum(lv[0:1] * lv[1:2], axis=1, keepdims=True))
               - jnp.exp(jnp.sum(lv[2:3] * lv[3:4], axis=1, keepdims=True)) + lam_init)
        o_ref[l] = jnp.broadcast_to(lam, o_ref.shape[1:])


def _attn_lambda(attn_lambda, lam_inits):
    depth = attn_lambda.shape[0]
    return pl.pallas_call(
        functools.partial(_attn_lambda_kernel, lam_inits=lam_inits),
        out_shape=jax.ShapeDtypeStruct((depth, 1, TM), F32),
        name="attn_lambda",
    )(attn_lambda)


def _attn_kernel(lam_ref, qt0_ref, qt1_ref, k_ref, vt_ref, o_ref, *, n_ctx, lam_init):
    i = pl.program_id(2)
    streams = [(hd, qt_ref[0, hd * V_DIM:(hd + 1) * V_DIM, :])
               for hd in range(HEADS_PER_STEP) for qt_ref in (qt0_ref, qt1_ref)]

    def attend(n_keys):
        m = [None] * len(streams)
        u = [None] * len(streams)

        def scores(c0):
            return [jnp.dot(k_ref[0, c0:c0 + KEY_CHUNK, hd * V_DIM:(hd + 1) * V_DIM], qt,
                            preferred_element_type=F32) for hd, qt in streams]

        starts = list(range(0, n_keys, KEY_CHUNK))
        pending = [scores(c0) for c0 in starts[:SCORE_LOOKAHEAD]]
        for n, c0 in enumerate(starts):
            if n + SCORE_LOOKAHEAD < len(starts):
                pending.append(scores(starts[n + SCORE_LOOKAHEAD]))
            sts = pending.pop(0)
            for j, (hd, _) in enumerate(streams):
                m_chunk = jnp.max(sts[j], axis=0, keepdims=True)
                m_new = m_chunk if m[j] is None else jnp.maximum(m[j], m_chunk)
                pt = jnp.exp2(sts[j] - m_new).astype(BF16)
                part = jnp.dot(vt_ref[0, hd * VT_ROWS:(hd + 1) * VT_ROWS, c0:c0 + KEY_CHUNK], pt,
                               preferred_element_type=F32)
                u[j] = part if u[j] is None else u[j] * jnp.exp2(m[j] - m_new) + part
                m[j] = m_new
        for hd in range(HEADS_PER_STEP):
            u0, u1 = u[2 * hd], u[2 * hd + 1]
            ot = u0[:V_DIM] * (1.0 / u0[V_DIM:V_DIM + 1]) - u1[:V_DIM] * (lam_ref[...] / u1[V_DIM:V_DIM + 1])
            ot = ot * lax.rsqrt(jnp.mean(ot * ot, axis=0, keepdims=True) + EPS) * (1.0 - lam_init)
            o_ref[0, :, hd * V_DIM:(hd + 1) * V_DIM] = ot.T.astype(BF16)

    @pl.when(i == 0)
    def _():
        attend(n_ctx)

    @pl.when(i > 0)
    def _():
        attend(k_ref.shape[1])


def _attention(lam_row, qt0, qt1, k, vt, n_ctx, lam_init):
    b, t, d = k.shape
    assert n_ctx == TM
    hps = HEADS_PER_STEP
    qt = pl.BlockSpec((1, hps * V_DIM, TM), lambda bb, hh, i: (bb, hh, i))
    keys = pl.BlockSpec((1, t, hps * V_DIM), lambda bb, hh, i: (bb, 0, hh))
    vals = pl.BlockSpec((1, hps * VT_ROWS, t), lambda bb, hh, i: (bb, hh, 0))
    return pl.pallas_call(
        functools.partial(_attn_kernel, n_ctx=n_ctx, lam_init=lam_init),
        grid=(b, d // (hps * V_DIM), t // TM),
        in_specs=[pl.BlockSpec(lam_row.shape, lambda bb, hh, i: (0, 0)), qt, qt, keys, vals],
        out_specs=pl.BlockSpec((1, TM, hps * V_DIM), lambda bb, hh, i: (bb, i, hh)),
        out_shape=jax.ShapeDtypeStruct((b, t, d), BF16),
        compiler_params=_params(3),
        name="diffattn",
    )(lam_row, qt0, qt1, k, vt)


def _merge_kernel(x_ref, mod_ref, h_ref, ya_ref, yb_ref, yc_ref, wb_ref, wg_ref, bg_ref, wo_ref, o_ref):
    d = x_ref.shape[-1]
    g = jax.nn.sigmoid(jnp.dot(h_ref[0], wg_ref[...], preferred_element_type=F32) + bg_ref[...])
    m = g[:, :d] * jnp.dot(ya_ref[0], wb_ref[:d, :], preferred_element_type=F32)
    m = m + g[:, d:2 * d] * jnp.dot(yb_ref[0], wb_ref[d:2 * d, :], preferred_element_type=F32)
    m = m + g[:, 2 * d:] * jnp.dot(yc_ref[0], wb_ref[2 * d:, :], preferred_element_type=F32)
    out = jnp.dot(m.astype(BF16), wo_ref[...], preferred_element_type=F32)
    o_ref[0] = x_ref[0] + mod_ref[0, 2:3, :] * out


def _merge(x_all, mod, h, ya, yb, yc, w_branch, w_gate, b_gate, w_out, tile0):
    b, t, d = x_all.shape
    nt = t // TM - tile0
    tok = pl.BlockSpec((1, TM, d), lambda bb, i: (bb, i + tile0, 0))
    return pl.pallas_call(
        _merge_kernel,
        grid=(b, nt),
        in_specs=[
            tok,
            pl.BlockSpec((1, N_MOD, d), lambda bb, i: (jnp.where(i + tile0 == 0, b, bb), 0, 0)),
            tok, tok, tok, tok,
            _const_spec(w_branch.shape), _const_spec(w_gate.shape), _const_spec(b_gate.shape),
            _const_spec(w_out.shape),
        ],
        out_specs=pl.BlockSpec((1, TM, d), lambda bb, i: (bb, i, 0)),
        out_shape=jax.ShapeDtypeStruct((b, nt * TM, d), F32),
        compiler_params=_params(2),
        name="merge",
    )(x_all, mod, h, ya, yb, yc, w_branch, w_gate, b_gate, w_out)


def _ffn_kernel(x_ref, mod_ref, w1_ref, w3_ref, w2_ref, fg_ref, o_ref, *, final_norm):
    x = x_ref[0]
    hb = (_rms(x) * (1.0 + mod_ref[0, 4:5, :]) + mod_ref[0, 3:4, :]).astype(BF16)
    a = jnp.dot(hb, w1_ref[...], preferred_element_type=F32)
    g = jnp.dot(hb, w3_ref[...], preferred_element_type=F32)
    s = (a * jax.nn.sigmoid(a) * g).astype(BF16)
    y = x + mod_ref[0, 5:6, :] * jnp.dot(s, w2_ref[...], preferred_element_type=F32)
    if final_norm:
        y = _rms(y) * fg_ref[...]
        n_piece, steps, d = o_ref.shape[1:]
        o_ref[0] = pltpu.einshape("jsc->sjc", y.reshape(steps, n_piece, d))
    else:
        o_ref[0] = y


def _ffn(x_all, mod, w1, w3, w2, final_g, ctx_tiles, final_norm):
    b, t, d = x_all.shape
    tok = pl.BlockSpec((1, TM, d), lambda bb, i: (bb, i, 0))
    if final_norm:
        steps = TM // SUBLANES
        out_spec = pl.BlockSpec((1, SUBLANES, steps, d), lambda bb, i: (bb, 0, i, 0))
        out_shape = jax.ShapeDtypeStruct((b, SUBLANES, t // SUBLANES, d), F32)
    else:
        out_spec, out_shape = tok, jax.ShapeDtypeStruct((b, t, d), F32)
    out = pl.pallas_call(
        functools.partial(_ffn_kernel, final_norm=final_norm),
        grid=(b, t // TM),
        in_specs=[
            tok,
            pl.BlockSpec((1, N_MOD, d), lambda bb, i: (jnp.where(i < ctx_tiles, b, bb), 0, 0)),
            _const_spec(w1.shape), _const_spec(w3.shape), _const_spec(w2.shape),
            _const_spec(final_g.shape),
        ],
        out_specs=out_spec,
        out_shape=out_shape,
        compiler_params=_params(2),
        name="ffn",
    )(x_all, mod, w1, w3, w2, final_g)
    return out.reshape(b, t, d)


def _piece_order(n):
    r = np.arange(n, dtype=np.int64)
    return (r % SUBLANES) * (n // SUBLANES) + r // SUBLANES


def _to_piece_order(v):
    b, n, d = v.shape
    return v.reshape(b, SUBLANES, n // SUBLANES, d).swapaxes(1, 2).reshape(b, n, d)


def _from_piece_order(v):
    b, n, d = v.shape
    return v.reshape(b, n // SUBLANES, SUBLANES, d).swapaxes(1, 2).reshape(b, n, d)


def _rope_tables(n_ctx, n_lat):
    p = _piece_order(n_lat)
    rows = (p // GRID_W).astype(np.float32)
    cols = (p % GRID_W).astype(np.float32)
    n_freq = HEAD_DIM // 4
    inv = np.float32(ROPE_BASE) ** (-np.arange(n_freq, dtype=np.float32) / np.float32(n_freq))
    ang = np.concatenate([rows[:, None] * inv, cols[:, None] * inv], axis=-1)
    cos, sin = np.cos(ang), np.sin(ang)
    reps = LANES // HEAD_DIM
    cos_l = np.tile(np.concatenate([cos, cos], axis=-1), (1, reps))
    sin_l = np.tile(np.concatenate([-sin, sin], axis=-1), (1, reps))
    cos_t = np.concatenate([np.ones((n_ctx, LANES), np.float32), cos_l], axis=0)
    sin_t = np.concatenate([np.zeros((n_ctx, LANES), np.float32), sin_l], axis=0)
    return jnp.asarray(cos_t, F32), jnp.asarray(sin_t, F32)


def _dft_cos_sin(n, k):
    ang = ((k[:, None] * k[None, :]) % n) * (2.0 * math.pi / n)
    return np.cos(ang), np.sin(ang)


def _dft_position_matrix(n):
    c, s = _dft_cos_sin(n, _piece_order(n))
    return jnp.asarray(np.concatenate([c, -s], axis=1).astype(BF16))


def kernel(x, c, ctx, c_ctx, ada_w, ada_b, w_in, rnn_conv_w, rnn_conv_b, rnn_wr, rnn_br, rnn_wi, rnn_bi,
           rnn_lambda, attn_lambda, w_branch, w_gate, b_gate, w_out, ffn_w1, ffn_w3, ffn_w2, final_g):
    b, n_lat, d = x.shape
    n_ctx = ctx.shape[1]
    depth = ada_w.shape[0]
    ctx_tiles = n_ctx // TM
    assert n_ctx == TM and n_lat % TM == 0

    cond = jnp.zeros((2 * SUBLANES, d), F32).at[:b].set(c).at[b].set(c_ctx)
    mod_all = _adaln(cond, ada_w, ada_b.reshape(depth, 1, N_MOD * d))

    cos_t, sin_t = _rope_tables(n_ctx, n_lat)
    cc, sc = _dft_cos_sin(FOURIER_GD, np.arange(FOURIER_GD, dtype=np.int64))
    wf = jnp.asarray(np.concatenate([cc, sc], axis=1).astype(BF16))
    a_lat = _dft_position_matrix(n_lat)
    a_ctx = _dft_position_matrix(n_ctx)
    fg = final_g.reshape(1, d)

    x_all = jnp.concatenate([_to_piece_order(ctx), _to_piece_order(x)], axis=1)
    lam_inits = tuple(0.8 - 0.6 * math.exp(-0.3 * l) for l in range(depth))
    lam_rows = _attn_lambda(attn_lambda, lam_inits)
    for l in range(depth):
        last = l == depth - 1
        mod = mod_all[l].reshape(2 * SUBLANES, N_MOD, d)
        h, xr, gr, xc, xs, qt0, qt1, k, vt = _inproj(x_all, mod, w_in[l].astype(BF16), wf, cos_t, sin_t)
        ya = _rglru(xr, gr, rnn_conv_w[l], rnn_conv_b[l].reshape(1, d), rnn_wr[l].astype(BF16), rnn_br[l],
                    rnn_wi[l].astype(BF16), rnn_bi[l], rnn_lambda[l], n_ctx)
        yb = _fourier(a_lat, a_ctx, xc, xs, n_ctx)
        yc = _attention(lam_rows[l], qt0, qt1, k, vt, n_ctx, lam_inits[l])
        tile0 = ctx_tiles if last else 0
        x_mid = _merge(x_all, mod, h, ya, yb, yc, w_branch[l].astype(BF16), w_gate[l].astype(BF16),
                       b_gate[l].reshape(1, -1), w_out[l].astype(BF16), tile0)
        x_all = _ffn(x_mid, mod, ffn_w1[l].astype(BF16), ffn_w3[l].astype(BF16), ffn_w2[l].astype(BF16),
                     fg, ctx_tiles - tile0, last)
    return x_all
```

```python
import functools
import math

import jax
import jax.numpy as jnp
import numpy as np
from jax import lax
from jax.experimental import pallas as pl
from jax.experimental.pallas import tpu as pltpu

F32 = jnp.float32
BF16 = jnp.bfloat16

EPS = 1e-6
N_MOD = 6
GRID_W = 64
RNN_BLOCK = 128
CONV_W = 4
LRU_C = 8.0
FOURIER_GD = 128
N_HEADS = 8
HEAD_DIM = 64
V_DIM = 2 * HEAD_DIM
ROPE_BASE = 10000.0

LANES = 128
SUBLANES = 8
TM = 256
SCAN_UNROLL = 8
KEY_CHUNK = 256
SCORE_LOOKAHEAD = 3
HEADS_PER_STEP = 4
BF16_SUBLANES = 16
VT_ROWS = V_DIM + BF16_SUBLANES
LOG2_E = math.log2(math.e)
VMEM_LIMIT = 56 * 1024 * 1024


def _params(n_axes):
    return pltpu.CompilerParams(
        dimension_semantics=("parallel",) * n_axes, vmem_limit_bytes=VMEM_LIMIT)


def _const_spec(shape):
    nd = len(shape)
    return pl.BlockSpec(shape, lambda *_: (0,) * nd, pipeline_mode=pl.Buffered(1))


def _layer_spec(shape, layer):
    nd = len(shape)
    return pl.BlockSpec((1,) + tuple(shape[1:]), lambda *_: (layer,) + (0,) * (nd - 1),
                        pipeline_mode=pl.Buffered(1))


def _rms(x):
    return x * lax.rsqrt(jnp.mean(x * x, axis=-1, keepdims=True) + EPS)


def _adaln_kernel(c_ref, w_ref, b_ref, o_ref):
    c = c_ref[...]
    s = (c * jax.nn.sigmoid(c)).astype(BF16)
    o_ref[0] = jnp.dot(s, w_ref[0].astype(BF16), preferred_element_type=F32) + b_ref[0]


def _adaln(cond, ada_w, ada_b):
    depth, d, n = ada_w.shape
    r = cond.shape[0]
    tn = 1536
    return pl.pallas_call(
        _adaln_kernel,
        grid=(depth, n // tn),
        in_specs=[
            pl.BlockSpec((r, d), lambda l, j: (0, 0)),
            pl.BlockSpec((1, d, tn), lambda l, j: (l, 0, j)),
            pl.BlockSpec((1, 1, tn), lambda l, j: (l, 0, j)),
        ],
        out_specs=pl.BlockSpec((1, r, tn), lambda l, j: (l, 0, j)),
        out_shape=jax.ShapeDtypeStruct((depth, r, n), F32),
        compiler_params=_params(2),
        name="adaln",
    )(cond, ada_w, ada_b)


def _inproj_kernel(*refs, from_time_order):
    if from_time_order:
        (ctx_ref, lat_ref, mod_ref, w_ref, wf_ref, cos_ref, sin_ref,
         x_ref, h_ref, xr_ref, gr_ref, xc_ref, xs_ref, qt0_ref, qt1_ref, k_ref, vt_ref) = refs
        blk = jnp.where(pl.program_id(1) == 0, ctx_ref[0], lat_ref[0])
        x = jnp.swapaxes(blk, 0, 1).reshape(x_ref.shape[1:])
        x_ref[0] = x
    else:
        (x_ref, mod_ref, w_ref, wf_ref, cos_ref, sin_ref,
         h_ref, xr_ref, gr_ref, xc_ref, xs_ref, qt0_ref, qt1_ref, k_ref, vt_ref) = refs
        x = x_ref[0]
    d = x.shape[-1]
    h = _rms(x) * (1.0 + mod_ref[0, 1:2, :]) + mod_ref[0, 0:1, :]
    hb = h.astype(BF16)
    h_ref[0] = hb

    def proj(j):
        return jnp.dot(hb, w_ref[0, :, j * d:(j + 1) * d].astype(BF16), preferred_element_type=F32)

    xr_ref[0] = proj(0)
    gr_ref[0] = proj(1)

    xf = proj(2).astype(BF16)
    for g in range(d // FOURIER_GD):
        sl = slice(g * FOURIER_GD, (g + 1) * FOURIER_GD)
        z = jnp.dot(xf[:, sl], wf_ref[...], preferred_element_type=F32)
        xc_ref[0, :, sl] = z[:, :FOURIER_GD].astype(BF16)
        xs_ref[0, :, sl] = z[:, FOURIER_GD:].astype(BF16)

    cos = cos_ref[...]
    sin = sin_ref[...]
    lane = lax.broadcasted_iota(jnp.int32, cos.shape, 1)
    first_half = (lane % HEAD_DIM) < (HEAD_DIM // 2)

    def rope(th):
        swapped = jnp.where(first_half,
                            pltpu.roll(th, LANES - HEAD_DIM // 2, 1),
                            pltpu.roll(th, HEAD_DIM // 2, 1))
        return th * cos + swapped * sin

    q = proj(3)
    row = lax.broadcasted_iota(jnp.int32, (V_DIM, q.shape[0]), 0)
    zero_t = jnp.zeros((V_DIM, q.shape[0]), F32)
    for hh in range(d // V_DIM):
        sl = slice(hh * V_DIM, (hh + 1) * V_DIM)
        qt = (rope(q[:, sl]) * (HEAD_DIM ** -0.5 * LOG2_E)).T
        qt0_ref[0, sl, :] = jnp.where(row < HEAD_DIM, qt, zero_t).astype(BF16)
        qt1_ref[0, sl, :] = jnp.where(row >= HEAD_DIM, qt, zero_t).astype(BF16)

    k = proj(4)
    for hh in range(d // LANES):
        sl = slice(hh * LANES, (hh + 1) * LANES)
        k_ref[0, :, sl] = rope(k[:, sl]).astype(BF16)

    v = proj(5)
    ones = jnp.ones((VT_ROWS - V_DIM, v.shape[0]), BF16)
    for hh in range(d // V_DIM):
        vt_ref[0, hh * VT_ROWS:hh * VT_ROWS + V_DIM, :] = v[:, hh * V_DIM:(hh + 1) * V_DIM].T.astype(BF16)
        vt_ref[0, hh * VT_ROWS + V_DIM:(hh + 1) * VT_ROWS, :] = ones


def _inproj(x_src, mod, w_in, layer, wf, cos_t, sin_t):
    from_time_order = isinstance(x_src, tuple)
    d = mod.shape[-1]
    tok = pl.BlockSpec((1, TM, d), lambda bb, i: (bb, i, 0))
    if from_time_order:
        ctx, x = x_src
        b, t = x.shape[0], ctx.shape[1] + x.shape[1]
        steps = TM // SUBLANES
        as_pieces = lambda v: v.reshape(b, SUBLANES, v.shape[1] // SUBLANES, d)
        srcs = [as_pieces(ctx), as_pieces(x)]
        src_specs = [pl.BlockSpec((1, SUBLANES, steps, d), lambda bb, i: (bb, 0, 0, 0)),
                     pl.BlockSpec((1, SUBLANES, steps, d), lambda bb, i: (bb, 0, jnp.maximum(i - 1, 0), 0))]
    else:
        b, t, _ = x_src.shape
        srcs, src_specs = [x_src], [tok]
    nt = t // TM
    vt_rows = d // V_DIM * VT_ROWS
    tok_t = lambda rows: pl.BlockSpec((1, rows, TM), lambda bb, i: (bb, 0, i))
    f32_out = jax.ShapeDtypeStruct((b, t, d), F32)
    bf_out = jax.ShapeDtypeStruct((b, t, d), BF16)
    extra = from_time_order * 1
    return pl.pallas_call(
        functools.partial(_inproj_kernel, from_time_order=from_time_order),
        grid=(b, nt),
        in_specs=src_specs + [
            pl.BlockSpec((1, N_MOD, d), lambda bb, i: (jnp.where(i == 0, b, bb), 0, 0)),
            _layer_spec(w_in.shape, layer),
            _const_spec(wf.shape),
            pl.BlockSpec((TM, LANES), lambda bb, i: (i, 0)),
            pl.BlockSpec((TM, LANES), lambda bb, i: (i, 0)),
        ],
        out_specs=[tok] * (5 + extra) + [tok_t(d), tok_t(d), tok, tok_t(vt_rows)],
        out_shape=[f32_out] * extra + [bf_out, f32_out, f32_out, bf_out, bf_out,
                                       jax.ShapeDtypeStruct((b, d, t), BF16),
                                       jax.ShapeDtypeStruct((b, d, t), BF16), bf_out,
                                       jax.ShapeDtypeStruct((b, vt_rows, t), BF16)],
        compiler_params=_params(2),
        name="inproj",
    )(*srcs, mod, w_in, wf, cos_t, sin_t)


def _log_sigmoid(x):
    return jnp.minimum(x, 0.0) - jnp.log(1.0 + jnp.exp(-jnp.abs(x)))


def _sigmoid(x):
    return 0.5 * jnp.tanh(0.5 * x) + 0.5


def _rglru_kernel(xr_ref, gr_ref, cw_ref, cb_ref, wr_ref, br_ref, wi_ref, bi_ref, lam_ref, ya_ref,
                  xpad, a0, b0, a1, b1, h0, h1, a20, b20, a21, b21, *, n_ctx):
    t_all, c = xr_ref.shape[1], xr_ref.shape[2]
    n_piece = SUBLANES
    w = [cw_ref[k:k + 1, :] for k in range(CONV_W)]
    sub = lax.broadcasted_iota(jnp.int32, (n_piece, c), 0)
    zeros = jnp.zeros((n_piece, c), F32)

    def from_prev_piece(v):
        return jnp.where(sub >= 1, pltpu.roll(v, 1, 0), 0.0)

    def from_next_piece(v):
        return jnp.where(sub < n_piece - 1, pltpu.roll(v, n_piece - 1, 0), 0.0)

    def run_sequence(base, n, state_f, state_r):
        steps = n // n_piece
        x3 = xr_ref[0, base:base + n, :].reshape(steps, n_piece, c)
        xpad[2:steps + 2] = x3
        xpad[0] = from_prev_piece(x3[steps - 2])
        xpad[1] = from_prev_piece(x3[steps - 1])
        xpad[steps + 2] = from_next_piece(x3[0])
        u = xpad[0:steps] * w[0]
        u = u + xpad[1:steps + 1] * w[1]
        u = u + xpad[2:steps + 2] * w[2]
        u = u + xpad[3:steps + 3] * w[3]
        u = (u + cb_ref[...]).reshape(n, c)
        ub = u.astype(BF16)
        totals = []
        for dd, (a_s, b_s, a2_s, b2_s) in enumerate(((a0, b0, a20, b20), (a1, b1, a21, b21))):
            r = _sigmoid(jnp.dot(ub, wr_ref[dd, 0], preferred_element_type=F32) + br_ref[dd:dd + 1, :])
            gate_i = _sigmoid(jnp.dot(ub, wi_ref[dd, 0], preferred_element_type=F32) + bi_ref[dd:dd + 1, :])
            a = jnp.exp2(r * ((LRU_C * LOG2_E) * _log_sigmoid(lam_ref[dd:dd + 1, :])))
            p = a.reshape(steps, n_piece, c)
            q = (jnp.sqrt(1.0 - a * a) * gate_i * u).reshape(steps, n_piece, c)
            a_s[0:steps] = p
            b_s[0:steps] = q
            m = steps
            while m > 1:
                p = p.reshape(m // 2, 2, n_piece, c)
                q = q.reshape(m // 2, 2, n_piece, c)
                first, second = (0, 1) if dd == 0 else (1, 0)
                q = p[:, second] * q[:, first] + q[:, second]
                p = p[:, second] * p[:, first]
                m //= 2
                if m == steps // 2:
                    a2_s[0:m] = p
                    b2_s[0:m] = q
            totals.append((p[0], q[0]))
        (pf, hf), (pr, hr) = totals

        init_f = zeros
        for s in range(n_piece):
            init_f = jnp.where(sub == s, state_f, init_f)
            state_f = hf[s:s + 1, :] + pf[s:s + 1, :] * state_f
        init_r = zeros
        for s in reversed(range(n_piece)):
            init_r = jnp.where(sub == s, state_r, init_r)
            state_r = hr[s:s + 1, :] + pr[s:s + 1, :] * state_r

        def piece_states(k, carry):
            hf, hr = carry
            h0[2 * k] = a0[2 * k] * hf + b0[2 * k]
            hf = a20[k] * hf + b20[k]
            h0[2 * k + 1] = hf
            kr = steps // 2 - 1 - k
            h1[2 * kr + 1] = a1[2 * kr + 1] * hr + b1[2 * kr + 1]
            hr = a21[kr] * hr + b21[kr]
            h1[2 * kr] = hr
            return hf, hr

        lax.fori_loop(0, steps // 2, piece_states, (init_f, init_r), unroll=SCAN_UNROLL)
        h = (h0[0:steps] + h1[0:steps]).reshape(n, c)
        ya_ref[0, base:base + n, :] = (h * jax.nn.gelu(gr_ref[0, base:base + n, :])).astype(BF16)
        return state_f, state_r

    zero_state = jnp.zeros((1, c), F32)
    state_f, state_r = run_sequence(0, n_ctx, zero_state, zero_state)
    run_sequence(n_ctx, t_all - n_ctx, state_f, state_r)


def _rglru(xr, gr, conv_w, conv_b, wr, br, wi, bi, lam, n_ctx):
    b, t, d = xr.shape
    c = RNN_BLOCK
    steps = max(n_ctx, t - n_ctx) // SUBLANES
    seq = pl.BlockSpec((1, t, c), lambda bb, g: (bb, 0, g))
    vec = lambda rows: pl.BlockSpec((rows, c), lambda bb, g: (0, g))
    mat = pl.BlockSpec((2, 1, c, c), lambda bb, g: (0, g, 0, 0))
    return pl.pallas_call(
        functools.partial(_rglru_kernel, n_ctx=n_ctx),
        grid=(b, d // c),
        in_specs=[seq, seq, vec(CONV_W), vec(1), mat, vec(2), mat, vec(2), vec(2)],
        out_specs=seq,
        out_shape=jax.ShapeDtypeStruct((b, t, d), BF16),
        scratch_shapes=([pltpu.VMEM((steps + CONV_W - 1, SUBLANES, c), F32)]
                        + [pltpu.VMEM((steps, SUBLANES, c), F32)] * 6
                        + [pltpu.VMEM((steps // 2, SUBLANES, c), F32)] * 4),
        compiler_params=_params(2),
        name="rglru",
    )(xr, gr, conv_w, conv_b, wr, br, wi, bi, lam)


def _fourier_kernel(al_ref, ac_ref, xc_ref, xs_ref, y_ref, *, n_ctx, scale_ctx, scale_lat):
    i = pl.program_id(1)
    n_lat = xc_ref.shape[1] - n_ctx

    def mix(a_ref, lo, n, scale):
        y = jnp.dot(a_ref[:, :n], xc_ref[0, lo:lo + n, :], preferred_element_type=F32)
        y = y + jnp.dot(a_ref[:, n:], xs_ref[0, lo:lo + n, :], preferred_element_type=F32)
        y_ref[0] = (y * scale).astype(BF16)

    @pl.when(i == 0)
    def _():
        mix(ac_ref, 0, n_ctx, scale_ctx)

    @pl.when(i > 0)
    def _():
        mix(al_ref, n_ctx, n_lat, scale_lat)


def _fourier(a_lat, a_ctx, xc, xs, n_ctx):
    b, t, d = xc.shape
    n_lat = t - n_ctx
    assert n_ctx == TM
    slab = pl.BlockSpec((1, t, d), lambda bb, i: (bb, 0, 0))
    return pl.pallas_call(
        functools.partial(_fourier_kernel, n_ctx=n_ctx,
                          scale_ctx=(n_ctx * FOURIER_GD) ** -0.5,
                          scale_lat=(n_lat * FOURIER_GD) ** -0.5),
        grid=(b, t // TM),
        in_specs=[
            pl.BlockSpec((TM, 2 * n_lat), lambda bb, i: (jnp.maximum(i - 1, 0), 0)),
            _const_spec(a_ctx.shape),
            slab, slab,
        ],
        out_specs=pl.BlockSpec((1, TM, d), lambda bb, i: (bb, i, 0)),
        out_shape=jax.ShapeDtypeStruct((b, t, d), BF16),
        compiler_params=_params(2),
        name="fourier",
    )(a_lat, a_ctx, xc, xs)


def _attn_lambda_kernel(lv_ref, o_ref, *, lam_inits):
    for l, lam_init in enumerate(lam_inits):
        lv = lv_ref[l]
        lam = (jnp.exp(jnp.sum(lv[0:1] * lv[1:2], axis=1, keepdims=True))
               - jnp.exp(jnp.sum(lv[2:3] * lv[3:4], axis=1, keepdims=True)) + lam_init)
        o_ref[l] = jnp.broadcast_to(lam, o_ref.shape[1:])


def _attn_lambda(attn_lambda, lam_inits):
    depth = attn_lambda.shape[0]
    return pl.pallas_call(
        functools.partial(_attn_lambda_kernel, lam_inits=lam_inits),
        out_shape=jax.ShapeDtypeStruct((depth, 1, TM), F32),
        name="attn_lambda",
    )(attn_lambda)


def _attn_kernel(lam_ref, qt0_ref, qt1_ref, k_ref, vt_ref, o_ref, *, n_ctx, lam_init):
    i = pl.program_id(2)
    streams = [(hd, qt_ref[0, hd * V_DIM:(hd + 1) * V_DIM, :])
               for hd in range(HEADS_PER_STEP) for qt_ref in (qt0_ref, qt1_ref)]

    def attend(n_keys):
        m = [None] * len(streams)
        u = [None] * len(streams)

        def scores(c0):
            return [jnp.dot(k_ref[0, c0:c0 + KEY_CHUNK, hd * V_DIM:(hd + 1) * V_DIM], qt,
                            preferred_element_type=F32) for hd, qt in streams]

        starts = list(range(0, n_keys, KEY_CHUNK))
        pending = [scores(c0) for c0 in starts[:SCORE_LOOKAHEAD]]
        for n, c0 in enumerate(starts):
            if n + SCORE_LOOKAHEAD < len(starts):
                pending.append(scores(starts[n + SCORE_LOOKAHEAD]))
            sts = pending.pop(0)
            for j, (hd, _) in enumerate(streams):
                m_chunk = jnp.max(sts[j], axis=0, keepdims=True)
                m_new = m_chunk if m[j] is None else jnp.maximum(m[j], m_chunk)
                pt = jnp.exp2(sts[j] - m_new).astype(BF16)
                part = jnp.dot(vt_ref[0, hd * VT_ROWS:(hd + 1) * VT_ROWS, c0:c0 + KEY_CHUNK], pt,
                               preferred_element_type=F32)
                u[j] = part if u[j] is None else u[j] * jnp.exp2(m[j] - m_new) + part
                m[j] = m_new
        for hd in range(HEADS_PER_STEP):
            u0, u1 = u[2 * hd], u[2 * hd + 1]
            ot = u0[:V_DIM] * (1.0 / u0[V_DIM:V_DIM + 1]) - u1[:V_DIM] * (lam_ref[...] / u1[V_DIM:V_DIM + 1])
            ot = ot * lax.rsqrt(jnp.mean(ot * ot, axis=0, keepdims=True) + EPS) * (1.0 - lam_init)
            o_ref[0, :, hd * V_DIM:(hd + 1) * V_DIM] = ot.T.astype(BF16)

    @pl.when(i == 0)
    def _():
        attend(n_ctx)

    @pl.when(i > 0)
    def _():
        attend(k_ref.shape[1])


def _attention(lam_row, qt0, qt1, k, vt, n_ctx, lam_init):
    b, t, d = k.shape
    assert n_ctx == TM
    hps = HEADS_PER_STEP
    qt = pl.BlockSpec((1, hps * V_DIM, TM), lambda bb, hh, i: (bb, hh, i))
    keys = pl.BlockSpec((1, t, hps * V_DIM), lambda bb, hh, i: (bb, 0, hh))
    vals = pl.BlockSpec((1, hps * VT_ROWS, t), lambda bb, hh, i: (bb, hh, 0))
    return pl.pallas_call(
        functools.partial(_attn_kernel, n_ctx=n_ctx, lam_init=lam_init),
        grid=(b, d // (hps * V_DIM), t // TM),
        in_specs=[pl.BlockSpec(lam_row.shape, lambda bb, hh, i: (0, 0)), qt, qt, keys, vals],
        out_specs=pl.BlockSpec((1, TM, hps * V_DIM), lambda bb, hh, i: (bb, i, hh)),
        out_shape=jax.ShapeDtypeStruct((b, t, d), BF16),
        compiler_params=_params(3),
        name="diffattn",
    )(lam_row, qt0, qt1, k, vt)


def _merge_kernel(x_ref, mod_ref, h_ref, ya_ref, yb_ref, yc_ref, wb_ref, wg_ref, bg_ref, wo_ref, o_ref):
    d = x_ref.shape[-1]
    g = jax.nn.sigmoid(jnp.dot(h_ref[0], wg_ref[0].astype(BF16), preferred_element_type=F32) + bg_ref[...])
    m = g[:, :d] * jnp.dot(ya_ref[0], wb_ref[0, :d, :].astype(BF16), preferred_element_type=F32)
    m = m + g[:, d:2 * d] * jnp.dot(yb_ref[0], wb_ref[0, d:2 * d, :].astype(BF16), preferred_element_type=F32)
    m = m + g[:, 2 * d:] * jnp.dot(yc_ref[0], wb_ref[0, 2 * d:, :].astype(BF16), preferred_element_type=F32)
    out = jnp.dot(m.astype(BF16), wo_ref[0].astype(BF16), preferred_element_type=F32)
    o_ref[0] = x_ref[0] + mod_ref[0, 2:3, :] * out


def _merge(x_all, mod, h, ya, yb, yc, w_branch, w_gate, b_gate, w_out, layer, tile0):
    b, t, d = x_all.shape
    nt = t // TM - tile0
    tok = pl.BlockSpec((1, TM, d), lambda bb, i: (bb, i + tile0, 0))
    return pl.pallas_call(
        _merge_kernel,
        grid=(b, nt),
        in_specs=[
            tok,
            pl.BlockSpec((1, N_MOD, d), lambda bb, i: (jnp.where(i + tile0 == 0, b, bb), 0, 0)),
            tok, tok, tok, tok,
            _layer_spec(w_branch.shape, layer), _layer_spec(w_gate.shape, layer), _const_spec(b_gate.shape),
            _layer_spec(w_out.shape, layer),
        ],
        out_specs=pl.BlockSpec((1, TM, d), lambda bb, i: (bb, i, 0)),
        out_shape=jax.ShapeDtypeStruct((b, nt * TM, d), F32),
        compiler_params=_params(2),
        name="merge",
    )(x_all, mod, h, ya, yb, yc, w_branch, w_gate, b_gate, w_out)


def _ffn_kernel(x_ref, mod_ref, w1_ref, w3_ref, w2_ref, fg_ref, o_ref, *, final_norm):
    x = x_ref[0]
    hb = (_rms(x) * (1.0 + mod_ref[0, 4:5, :]) + mod_ref[0, 3:4, :]).astype(BF16)
    a = jnp.dot(hb, w1_ref[0].astype(BF16), preferred_element_type=F32)
    g = jnp.dot(hb, w3_ref[0].astype(BF16), preferred_element_type=F32)
    s = (a * jax.nn.sigmoid(a) * g).astype(BF16)
    y = x + mod_ref[0, 5:6, :] * jnp.dot(s, w2_ref[0].astype(BF16), preferred_element_type=F32)
    if final_norm:
        y = _rms(y) * fg_ref[...]
        n_piece, steps, d = o_ref.shape[1:]
        o_ref[0] = jnp.swapaxes(y.reshape(steps, n_piece, d), 0, 1)
    else:
        o_ref[0] = y


def _ffn(x_all, mod, w1, w3, w2, layer, final_g, ctx_tiles, final_norm):
    b, t, d = x_all.shape
    tok = pl.BlockSpec((1, TM, d), lambda bb, i: (bb, i, 0))
    if final_norm:
        steps = TM // SUBLANES
        out_spec = pl.BlockSpec((1, SUBLANES, steps, d), lambda bb, i: (bb, 0, i, 0))
        out_shape = jax.ShapeDtypeStruct((b, SUBLANES, t // SUBLANES, d), F32)
    else:
        out_spec, out_shape = tok, jax.ShapeDtypeStruct((b, t, d), F32)
    out = pl.pallas_call(
        functools.partial(_ffn_kernel, final_norm=final_norm),
        grid=(b, t // TM),
        in_specs=[
            tok,
            pl.BlockSpec((1, N_MOD, d), lambda bb, i: (jnp.where(i < ctx_tiles, b, bb), 0, 0)),
            _layer_spec(w1.shape, layer), _layer_spec(w3.shape, layer), _layer_spec(w2.shape, layer),
            _const_spec(final_g.shape),
        ],
        out_specs=out_spec,
        out_shape=out_shape,
        compiler_params=_params(2),
        name="ffn",
    )(x_all, mod, w1, w3, w2, final_g)
    return out.reshape(b, t, d)


def _piece_order(n):
    r = np.arange(n, dtype=np.int64)
    return (r % SUBLANES) * (n // SUBLANES) + r // SUBLANES


def _rope_tables(n_ctx, n_lat):
    p = _piece_order(n_lat)
    rows = (p // GRID_W).astype(np.float32)
    cols = (p % GRID_W).astype(np.float32)
    n_freq = HEAD_DIM // 4
    inv = np.float32(ROPE_BASE) ** (-np.arange(n_freq, dtype=np.float32) / np.float32(n_freq))
    ang = np.concatenate([rows[:, None] * inv, cols[:, None] * inv], axis=-1)
    cos, sin = np.cos(ang), np.sin(ang)
    reps = LANES // HEAD_DIM
    cos_l = np.tile(np.concatenate([cos, cos], axis=-1), (1, reps))
    sin_l = np.tile(np.concatenate([-sin, sin], axis=-1), (1, reps))
    cos_t = np.concatenate([np.ones((n_ctx, LANES), np.float32), cos_l], axis=0)
    sin_t = np.concatenate([np.zeros((n_ctx, LANES), np.float32), sin_l], axis=0)
    return jnp.asarray(cos_t, F32), jnp.asarray(sin_t, F32)


def _dft_cos_sin(n, k):
    ang = ((k[:, None] * k[None, :]) % n) * (2.0 * math.pi / n)
    return np.cos(ang), np.sin(ang)


def _dft_position_matrix(n):
    c, s = _dft_cos_sin(n, _piece_order(n))
    return jnp.asarray(np.concatenate([c, -s], axis=1), F32).astype(BF16)


def kernel(x, c, ctx, c_ctx, ada_w, ada_b, w_in, rnn_conv_w, rnn_conv_b, rnn_wr, rnn_br, rnn_wi, rnn_bi,
           rnn_lambda, attn_lambda, w_branch, w_gate, b_gate, w_out, ffn_w1, ffn_w3, ffn_w2, final_g):
    b, n_lat, d = x.shape
    n_ctx = ctx.shape[1]
    depth = ada_w.shape[0]
    ctx_tiles = n_ctx // TM
    assert n_ctx == TM and n_lat % TM == 0

    cond = jnp.zeros((2 * SUBLANES, d), F32).at[:b].set(c).at[b].set(c_ctx)
    mod_all = _adaln(cond, ada_w, ada_b.reshape(depth, 1, N_MOD * d))

    cos_t, sin_t = _rope_tables(n_ctx, n_lat)
    cc, sc = _dft_cos_sin(FOURIER_GD, np.arange(FOURIER_GD, dtype=np.int64))
    wf = jnp.asarray(np.concatenate([cc, sc], axis=1), F32).astype(BF16)
    a_lat = _dft_position_matrix(n_lat)
    a_ctx = _dft_position_matrix(n_ctx)
    fg = final_g.reshape(1, d)

    x_all = (ctx, x)
    lam_inits = tuple(0.8 - 0.6 * math.exp(-0.3 * l) for l in range(depth))
    lam_rows = _attn_lambda(attn_lambda, lam_inits)
    for l in range(depth):
        last = l == depth - 1
        mod = mod_all[l].reshape(2 * SUBLANES, N_MOD, d)
        outs = _inproj(x_all, mod, w_in, l, wf, cos_t, sin_t)
        if l == 0:
            x_all, outs = outs[0], outs[1:]
        h, xr, gr, xc, xs, qt0, qt1, k, vt = outs
        ya = _rglru(xr, gr, rnn_conv_w[l], rnn_conv_b[l].reshape(1, d), rnn_wr[l].astype(BF16), rnn_br[l],
                    rnn_wi[l].astype(BF16), rnn_bi[l], rnn_lambda[l], n_ctx)
        yb = _fourier(a_lat, a_ctx, xc, xs, n_ctx)
        yc = _attention(lam_rows[l], qt0, qt1, k, vt, n_ctx, lam_inits[l])
        tile0 = ctx_tiles if last else 0
        x_mid = _merge(x_all, mod, h, ya, yb, yc, w_branch, w_gate, b_gate[l].reshape(1, -1), w_out, l, tile0)
        x_all = _ffn(x_mid, mod, ffn_w1, ffn_w3, ffn_w2, l, fg, ctx_tiles - tile0, last)
    return x_all
```

```python
import functools
import math

import jax
import jax.numpy as jnp
import numpy as np
from jax import lax
from jax.experimental import pallas as pl
from jax.experimental.pallas import tpu as pltpu

F32 = jnp.float32
BF16 = jnp.bfloat16

EPS = 1e-6
N_MOD = 6
GRID_W = 64
RNN_BLOCK = 128
CONV_W = 4
LRU_C = 8.0
FOURIER_GD = 128
N_HEADS = 8
HEAD_DIM = 64
V_DIM = 2 * HEAD_DIM
ROPE_BASE = 10000.0

LANES = 128
SUBLANES = 8
TM = 256
SCAN_UNROLL = 8
KEY_CHUNK = 256
SCORE_LOOKAHEAD = 2
HEADS_PER_STEP = 4
BF16_SUBLANES = 16
VT_ROWS = V_DIM + BF16_SUBLANES
LOG2_E = math.log2(math.e)
VMEM_LIMIT = 56 * 1024 * 1024


def _params(n_axes):
    return pltpu.CompilerParams(
        dimension_semantics=("parallel",) * n_axes, vmem_limit_bytes=VMEM_LIMIT)


def _const_spec(shape):
    nd = len(shape)
    return pl.BlockSpec(shape, lambda *_: (0,) * nd, pipeline_mode=pl.Buffered(1))


def _layer_spec(shape, layer):
    nd = len(shape)
    return pl.BlockSpec((1,) + tuple(shape[1:]), lambda *_: (layer,) + (0,) * (nd - 1),
                        pipeline_mode=pl.Buffered(1))


def _rms(x):
    return x * lax.rsqrt(jnp.mean(x * x, axis=-1, keepdims=True) + EPS)


def _adaln_kernel(c_ref, w_ref, b_ref, o_ref):
    c = c_ref[...]
    s = (c * jax.nn.sigmoid(c)).astype(BF16)
    o_ref[0] = jnp.dot(s, w_ref[0].astype(BF16), preferred_element_type=F32) + b_ref[0]


def _adaln(cond, ada_w, ada_b):
    depth, d, n = ada_w.shape
    r = cond.shape[0]
    tn = 1536
    return pl.pallas_call(
        _adaln_kernel,
        grid=(depth, n // tn),
        in_specs=[
            pl.BlockSpec((r, d), lambda l, j: (0, 0)),
            pl.BlockSpec((1, d, tn), lambda l, j: (l, 0, j)),
            pl.BlockSpec((1, 1, tn), lambda l, j: (l, 0, j)),
        ],
        out_specs=pl.BlockSpec((1, r, tn), lambda l, j: (l, 0, j)),
        out_shape=jax.ShapeDtypeStruct((depth, r, n), F32),
        compiler_params=_params(2),
        name="adaln",
    )(cond, ada_w, ada_b)


def _inproj_kernel(*refs, from_time_order):
    if from_time_order:
        (ctx_ref, lat_ref, mod_ref, w_ref, wf_ref, cos_ref, sin_ref,
         x_ref, h_ref, xr_ref, gr_ref, xce_ref, xse_ref, xco_ref, xso_ref, qt0_ref, qt1_ref, k_ref, vt_ref) = refs
        blk = jnp.where(pl.program_id(1) == 0, ctx_ref[0], lat_ref[0])
        x = jnp.swapaxes(blk, 0, 1).reshape(x_ref.shape[1:])
        x_ref[0] = x
    else:
        (x_ref, mod_ref, w_ref, wf_ref, cos_ref, sin_ref,
         h_ref, xr_ref, gr_ref, xce_ref, xse_ref, xco_ref, xso_ref, qt0_ref, qt1_ref, k_ref, vt_ref) = refs
        x = x_ref[0]
    d = x.shape[-1]
    h = _rms(x) * (1.0 + mod_ref[0, 1:2, :]) + mod_ref[0, 0:1, :]
    hb = h.astype(BF16)
    h_ref[0] = hb

    def proj(j):
        return jnp.dot(hb, w_ref[0, :, j * d:(j + 1) * d].astype(BF16), preferred_element_type=F32)

    xr_ref[0] = proj(0)
    gr_ref[0] = proj(1)

    xf = proj(2).astype(BF16)
    steps = x.shape[0] // SUBLANES
    low = lax.broadcasted_iota(jnp.int32, (SUBLANES, 2 * FOURIER_GD), 0) < SUBLANES // 2
    for g in range(d // FOURIER_GD):
        sl = slice(g * FOURIER_GD, (g + 1) * FOURIER_GD)
        z = jnp.dot(xf[:, sl], wf_ref[...], preferred_element_type=F32)
        z3 = z.reshape(steps, SUBLANES, 2 * FOURIER_GD)
        zr = pltpu.roll(z3, SUBLANES // 2, 1)
        for folded, c_ref, s_ref in ((z3 + zr, xce_ref, xse_ref), (z3 - zr, xco_ref, xso_ref)):
            pair = folded.reshape(steps // 2, 2, SUBLANES, 2 * FOURIER_GD)
            kept = jnp.where(low, pair[:, 0], pair[:, 1]).reshape(x.shape[0] // 2, 2 * FOURIER_GD)
            c_ref[0, :, sl] = kept[:, :FOURIER_GD].astype(BF16)
            s_ref[0, :, sl] = kept[:, FOURIER_GD:].astype(BF16)

    cos = cos_ref[...]
    sin = sin_ref[...]
    lane = lax.broadcasted_iota(jnp.int32, cos.shape, 1)
    first_half = (lane % HEAD_DIM) < (HEAD_DIM // 2)

    def rope(th):
        swapped = jnp.where(first_half,
                            pltpu.roll(th, LANES - HEAD_DIM // 2, 1),
                            pltpu.roll(th, HEAD_DIM // 2, 1))
        return th * cos + swapped * sin

    q = proj(3)
    row = lax.broadcasted_iota(jnp.int32, (V_DIM, q.shape[0]), 0)
    zero_t = jnp.zeros((V_DIM, q.shape[0]), F32)
    for hh in range(d // V_DIM):
        sl = slice(hh * V_DIM, (hh + 1) * V_DIM)
        qt = (rope(q[:, sl]) * (HEAD_DIM ** -0.5 * LOG2_E)).T
        qt0_ref[0, sl, :] = jnp.where(row < HEAD_DIM, qt, zero_t).astype(BF16)
        qt1_ref[0, sl, :] = jnp.where(row >= HEAD_DIM, qt, zero_t).astype(BF16)

    k = proj(4)
    for hh in range(d // LANES):
        sl = slice(hh * LANES, (hh + 1) * LANES)
        k_ref[0, :, sl] = rope(k[:, sl]).astype(BF16)

    v = proj(5)
    ones = jnp.ones((VT_ROWS - V_DIM, v.shape[0]), BF16)
    for hh in range(d // V_DIM):
        vt_ref[0, hh * VT_ROWS:hh * VT_ROWS + V_DIM, :] = v[:, hh * V_DIM:(hh + 1) * V_DIM].T.astype(BF16)
        vt_ref[0, hh * VT_ROWS + V_DIM:(hh + 1) * VT_ROWS, :] = ones


def _inproj(x_src, mod, w_in, layer, wf, cos_t, sin_t):
    from_time_order = isinstance(x_src, tuple)
    d = mod.shape[-1]
    tok = pl.BlockSpec((1, TM, d), lambda bb, i: (bb, i, 0))
    if from_time_order:
        ctx, x = x_src
        b, t = x.shape[0], ctx.shape[1] + x.shape[1]
        steps = TM // SUBLANES
        as_pieces = lambda v: v.reshape(b, SUBLANES, v.shape[1] // SUBLANES, d)
        srcs = [as_pieces(ctx), as_pieces(x)]
        src_specs = [pl.BlockSpec((1, SUBLANES, steps, d), lambda bb, i: (bb, 0, 0, 0)),
                     pl.BlockSpec((1, SUBLANES, steps, d), lambda bb, i: (bb, 0, jnp.maximum(i - 1, 0), 0))]
    else:
        b, t, _ = x_src.shape
        srcs, src_specs = [x_src], [tok]
    nt = t // TM
    vt_rows = d // V_DIM * VT_ROWS
    tok_t = lambda rows: pl.BlockSpec((1, rows, TM), lambda bb, i: (bb, 0, i))
    f32_out = jax.ShapeDtypeStruct((b, t, d), F32)
    bf_out = jax.ShapeDtypeStruct((b, t, d), BF16)
    half_tok = pl.BlockSpec((1, TM // 2, d), lambda bb, i: (bb, i, 0))
    half_out = jax.ShapeDtypeStruct((b, t // 2, d), BF16)
    extra = from_time_order * 1
    return pl.pallas_call(
        functools.partial(_inproj_kernel, from_time_order=from_time_order),
        grid=(b, nt),
        in_specs=src_specs + [
            pl.BlockSpec((1, N_MOD, d), lambda bb, i: (jnp.where(i == 0, b, bb), 0, 0)),
            _layer_spec(w_in.shape, layer),
            _const_spec(wf.shape),
            pl.BlockSpec((TM, LANES), lambda bb, i: (i, 0)),
            pl.BlockSpec((TM, LANES), lambda bb, i: (i, 0)),
        ],
        out_specs=[tok] * (3 + extra) + [half_tok] * 4 + [tok_t(d), tok_t(d), tok, tok_t(vt_rows)],
        out_shape=[f32_out] * extra + [bf_out, f32_out, f32_out] + [half_out] * 4 + [
                                       jax.ShapeDtypeStruct((b, d, t), BF16),
                                       jax.ShapeDtypeStruct((b, d, t), BF16), bf_out,
                                       jax.ShapeDtypeStruct((b, vt_rows, t), BF16)],
        compiler_params=_params(2),
        name="inproj",
    )(*srcs, mod, w_in, wf, cos_t, sin_t)


def _log_sigmoid(x):
    return jnp.minimum(x, 0.0) - jnp.log(1.0 + jnp.exp(-jnp.abs(x)))


def _sigmoid(x):
    return 0.5 * jnp.tanh(0.5 * x) + 0.5


def _rglru_kernel(xr_ref, gr_ref, cw_ref, cb_ref, wr_ref, br_ref, wi_ref, bi_ref, lam_ref, ya_ref,
                  xpad, a0, b0, a1, b1, h0, h1, a20, b20, a21, b21, *, n_ctx):
    t_all, c = xr_ref.shape[1], xr_ref.shape[2]
    n_piece = SUBLANES
    w = [cw_ref[k:k + 1, :] for k in range(CONV_W)]
    sub = lax.broadcasted_iota(jnp.int32, (n_piece, c), 0)
    zeros = jnp.zeros((n_piece, c), F32)

    def from_prev_piece(v):
        return jnp.where(sub >= 1, pltpu.roll(v, 1, 0), 0.0)

    def from_next_piece(v):
        return jnp.where(sub < n_piece - 1, pltpu.roll(v, n_piece - 1, 0), 0.0)

    def run_sequence(base, n, state_f, state_r):
        steps = n // n_piece
        x3 = xr_ref[0, base:base + n, :].reshape(steps, n_piece, c)
        xpad[2:steps + 2] = x3
        xpad[0] = from_prev_piece(x3[steps - 2])
        xpad[1] = from_prev_piece(x3[steps - 1])
        xpad[steps + 2] = from_next_piece(x3[0])
        u = xpad[0:steps] * w[0]
        u = u + xpad[1:steps + 1] * w[1]
        u = u + xpad[2:steps + 2] * w[2]
        u = u + xpad[3:steps + 3] * w[3]
        u = (u + cb_ref[...]).reshape(n, c)
        ub = u.astype(BF16)
        totals = []
        for dd, (a_s, b_s, a2_s, b2_s) in enumerate(((a0, b0, a20, b20), (a1, b1, a21, b21))):
            r = _sigmoid(jnp.dot(ub, wr_ref[dd, 0], preferred_element_type=F32) + br_ref[dd:dd + 1, :])
            gate_i = _sigmoid(jnp.dot(ub, wi_ref[dd, 0], preferred_element_type=F32) + bi_ref[dd:dd + 1, :])
            a = jnp.exp2(r * ((LRU_C * LOG2_E) * _log_sigmoid(lam_ref[dd:dd + 1, :])))
            p = a.reshape(steps, n_piece, c)
            q = (jnp.sqrt(1.0 - a * a) * gate_i * u).reshape(steps, n_piece, c)
            a_s[0:steps] = p
            b_s[0:steps] = q
            m = steps
            while m > 1:
                p = p.reshape(m // 2, 2, n_piece, c)
                q = q.reshape(m // 2, 2, n_piece, c)
                first, second = (0, 1) if dd == 0 else (1, 0)
                q = p[:, second] * q[:, first] + q[:, second]
                p = p[:, second] * p[:, first]
                m //= 2
                if m == steps // 2:
                    a2_s[0:m] = p
                    b2_s[0:m] = q
            totals.append((p[0], q[0]))
        (pf, hf), (pr, hr) = totals

        init_f = zeros
        for s in range(n_piece):
            init_f = jnp.where(sub == s, state_f, init_f)
            state_f = hf[s:s + 1, :] + pf[s:s + 1, :] * state_f
        init_r = zeros
        for s in reversed(range(n_piece)):
            init_r = jnp.where(sub == s, state_r, init_r)
            state_r = hr[s:s + 1, :] + pr[s:s + 1, :] * state_r

        def piece_states(k, carry):
            hf, hr = carry
            h0[2 * k] = a0[2 * k] * hf + b0[2 * k]
            hf = a20[k] * hf + b20[k]
            h0[2 * k + 1] = hf
            kr = steps // 2 - 1 - k
            h1[2 * kr + 1] = a1[2 * kr + 1] * hr + b1[2 * kr + 1]
            hr = a21[kr] * hr + b21[kr]
            h1[2 * kr] = hr
            return hf, hr

        lax.fori_loop(0, steps // 2, piece_states, (init_f, init_r), unroll=SCAN_UNROLL)
        h = (h0[0:steps] + h1[0:steps]).reshape(n, c)
        ya_ref[0, base:base + n, :] = (h * jax.nn.gelu(gr_ref[0, base:base + n, :])).astype(BF16)
        return state_f, state_r

    zero_state = jnp.zeros((1, c), F32)
    state_f, state_r = run_sequence(0, n_ctx, zero_state, zero_state)
    run_sequence(n_ctx, t_all - n_ctx, state_f, state_r)


def _rglru(xr, gr, conv_w, conv_b, wr, br, wi, bi, lam, n_ctx):
    b, t, d = xr.shape
    c = RNN_BLOCK
    steps = max(n_ctx, t - n_ctx) // SUBLANES
    seq = pl.BlockSpec((1, t, c), lambda bb, g: (bb, 0, g))
    vec = lambda rows: pl.BlockSpec((rows, c), lambda bb, g: (0, g))
    mat = pl.BlockSpec((2, 1, c, c), lambda bb, g: (0, g, 0, 0))
    return pl.pallas_call(
        functools.partial(_rglru_kernel, n_ctx=n_ctx),
        grid=(b, d // c),
        in_specs=[seq, seq, vec(CONV_W), vec(1), mat, vec(2), mat, vec(2), vec(2)],
        out_specs=seq,
        out_shape=jax.ShapeDtypeStruct((b, t, d), BF16),
        scratch_shapes=([pltpu.VMEM((steps + CONV_W - 1, SUBLANES, c), F32)]
                        + [pltpu.VMEM((steps, SUBLANES, c), F32)] * 6
                        + [pltpu.VMEM((steps // 2, SUBLANES, c), F32)] * 4),
        compiler_params=_params(2),
        name="rglru",
    )(xr, gr, conv_w, conv_b, wr, br, wi, bi, lam)


def _fourier_kernel(al_ref, ac_ref, xce_ref, xse_ref, xco_ref, xso_ref, y_ref, *, n_ctx, scale_ctx, scale_lat):
    i = pl.program_id(1)
    half_ctx = n_ctx // 2
    half_lat = xce_ref.shape[1] - half_ctx
    rows, d = y_ref.shape[1], y_ref.shape[2]

    def mix(a_ref, lo, n2, scale):
        by_parity = []
        for par, (c_ref, s_ref) in enumerate(((xce_ref, xse_ref), (xco_ref, xso_ref))):
            y = jnp.dot(a_ref[par, :, :n2], c_ref[0, lo:lo + n2, :], preferred_element_type=F32)
            y = y + jnp.dot(a_ref[par, :, n2:], s_ref[0, lo:lo + n2, :], preferred_element_type=F32)
            by_parity.append((y * scale).reshape(rows // (2 * SUBLANES), SUBLANES, d))
        y_ref[0] = jnp.stack(by_parity, axis=1).reshape(rows, d).astype(BF16)

    @pl.when(i == 0)
    def _():
        mix(ac_ref, 0, half_ctx, scale_ctx)

    @pl.when(i > 0)
    def _():
        mix(al_ref, half_ctx, half_lat, scale_lat)


def _fourier(a_lat, a_ctx, xce, xse, xco, xso, n_ctx):
    b, t2, d = xce.shape
    t = 2 * t2
    n_lat = t - n_ctx
    assert n_ctx == TM
    slab = pl.BlockSpec((1, t2, d), lambda bb, i: (bb, 0, 0))
    return pl.pallas_call(
        functools.partial(_fourier_kernel, n_ctx=n_ctx,
                          scale_ctx=(n_ctx * FOURIER_GD) ** -0.5,
                          scale_lat=(n_lat * FOURIER_GD) ** -0.5),
        grid=(b, t // TM),
        in_specs=[
            pl.BlockSpec((2, TM // 2, n_lat), lambda bb, i: (0, jnp.maximum(i - 1, 0), 0)),
            _const_spec(a_ctx.shape),
            slab, slab, slab, slab,
        ],
        out_specs=pl.BlockSpec((1, TM, d), lambda bb, i: (bb, i, 0)),
        out_shape=jax.ShapeDtypeStruct((b, t, d), BF16),
        compiler_params=_params(2),
        name="fourier",
    )(a_lat, a_ctx, xce, xse, xco, xso)


def _attn_lambda_kernel(lv_ref, o_ref, *, lam_inits):
    for l, lam_init in enumerate(lam_inits):
        lv = lv_ref[l]
        lam = (jnp.exp(jnp.sum(lv[0:1] * lv[1:2], axis=1, keepdims=True))
               - jnp.exp(jnp.sum(lv[2:3] * lv[3:4], axis=1, keepdims=True)) + lam_init)
        o_ref[l] = jnp.broadcast_to(lam, o_ref.shape[1:])


def _attn_lambda(attn_lambda, lam_inits):
    depth = attn_lambda.shape[0]
    return pl.pallas_call(
        functools.partial(_attn_lambda_kernel, lam_inits=lam_inits),
        out_shape=jax.ShapeDtypeStruct((depth, 1, TM), F32),
        name="attn_lambda",
    )(attn_lambda)


def _attn_kernel(lam_ref, qt0_ref, qt1_ref, k_ref, vt_ref, o_ref, *, n_ctx, lam_init):
    i = pl.program_id(2)
    streams = [(hd, qt_ref[0, hd * V_DIM:(hd + 1) * V_DIM, :])
               for hd in range(HEADS_PER_STEP) for qt_ref in (qt0_ref, qt1_ref)]

    def attend(n_keys):
        m = [None] * len(streams)
        u = [None] * len(streams)

        def scores(c0):
            return [jnp.dot(k_ref[0, c0:c0 + KEY_CHUNK, hd * V_DIM:(hd + 1) * V_DIM], qt,
                            preferred_element_type=F32) for hd, qt in streams]

        starts = list(range(0, n_keys, KEY_CHUNK))
        pending = [scores(c0) for c0 in starts[:SCORE_LOOKAHEAD]]
        for n, c0 in enumerate(starts):
            if n + SCORE_LOOKAHEAD < len(starts):
                pending.append(scores(starts[n + SCORE_LOOKAHEAD]))
            sts = pending.pop(0)
            for j, (hd, _) in enumerate(streams):
                m_chunk = jnp.max(sts[j], axis=0, keepdims=True)
                m_new = m_chunk if m[j] is None else jnp.maximum(m[j], m_chunk)
                pt = jnp.exp2(sts[j] - m_new).astype(BF16)
                part = jnp.dot(vt_ref[0, hd * VT_ROWS:(hd + 1) * VT_ROWS, c0:c0 + KEY_CHUNK], pt,
                               preferred_element_type=F32)
                u[j] = part if u[j] is None else u[j] * jnp.exp2(m[j] - m_new) + part
                m[j] = m_new
        for hd in range(HEADS_PER_STEP):
            u0, u1 = u[2 * hd], u[2 * hd + 1]
            ot = u0[:V_DIM] * (1.0 / u0[V_DIM:V_DIM + 1]) - u1[:V_DIM] * (lam_ref[...] / u1[V_DIM:V_DIM + 1])
            ot = ot * lax.rsqrt(jnp.mean(ot * ot, axis=0, keepdims=True) + EPS) * (1.0 - lam_init)
            o_ref[0, :, hd * V_DIM:(hd + 1) * V_DIM] = ot.T.astype(BF16)

    @pl.when(i == 0)
    def _():
        attend(n_ctx)

    @pl.when(i > 0)
    def _():
        attend(k_ref.shape[1])


def _attention(lam_row, qt0, qt1, k, vt, n_ctx, lam_init):
    b, t, d = k.shape
    assert n_ctx == TM
    hps = HEADS_PER_STEP
    qt = pl.BlockSpec((1, hps * V_DIM, TM), lambda bb, hh, i: (bb, hh, i))
    keys = pl.BlockSpec((1, t, hps * V_DIM), lambda bb, hh, i: (bb, 0, hh))
    vals = pl.BlockSpec((1, hps * VT_ROWS, t), lambda bb, hh, i: (bb, hh, 0))
    return pl.pallas_call(
        functools.partial(_attn_kernel, n_ctx=n_ctx, lam_init=lam_init),
        grid=(b, d // (hps * V_DIM), t // TM),
        in_specs=[pl.BlockSpec(lam_row.shape, lambda bb, hh, i: (0, 0)), qt, qt, keys, vals],
        out_specs=pl.BlockSpec((1, TM, hps * V_DIM), lambda bb, hh, i: (bb, i, hh)),
        out_shape=jax.ShapeDtypeStruct((b, t, d), BF16),
        compiler_params=_params(3),
        name="diffattn",
    )(lam_row, qt0, qt1, k, vt)


def _merge_kernel(x_ref, mod_ref, h_ref, ya_ref, yb_ref, yc_ref, wb_ref, wg_ref, bg_ref, wo_ref, o_ref):
    d = x_ref.shape[-1]
    g = jax.nn.sigmoid(jnp.dot(h_ref[0], wg_ref[0].astype(BF16), preferred_element_type=F32) + bg_ref[...])
    m = g[:, :d] * jnp.dot(ya_ref[0], wb_ref[0, :d, :].astype(BF16), preferred_element_type=F32)
    m = m + g[:, d:2 * d] * jnp.dot(yb_ref[0], wb_ref[0, d:2 * d, :].astype(BF16), preferred_element_type=F32)
    m = m + g[:, 2 * d:] * jnp.dot(yc_ref[0], wb_ref[0, 2 * d:, :].astype(BF16), preferred_element_type=F32)
    out = jnp.dot(m.astype(BF16), wo_ref[0].astype(BF16), preferred_element_type=F32)
    o_ref[0] = x_ref[0] + mod_ref[0, 2:3, :] * out


def _merge(x_all, mod, h, ya, yb, yc, w_branch, w_gate, b_gate, w_out, layer, tile0):
    b, t, d = x_all.shape
    nt = t // TM - tile0
    tok = pl.BlockSpec((1, TM, d), lambda bb, i: (bb, i + tile0, 0))
    return pl.pallas_call(
        _merge_kernel,
        grid=(b, nt),
        in_specs=[
            tok,
            pl.BlockSpec((1, N_MOD, d), lambda bb, i: (jnp.where(i + tile0 == 0, b, bb), 0, 0)),
            tok, tok, tok, tok,
            _layer_spec(w_branch.shape, layer), _layer_spec(w_gate.shape, layer), _const_spec(b_gate.shape),
            _layer_spec(w_out.shape, layer),
        ],
        out_specs=pl.BlockSpec((1, TM, d), lambda bb, i: (bb, i, 0)),
        out_shape=jax.ShapeDtypeStruct((b, nt * TM, d), F32),
        compiler_params=_params(2),
        name="merge",
    )(x_all, mod, h, ya, yb, yc, w_branch, w_gate, b_gate, w_out)


def _ffn_kernel(x_ref, mod_ref, w1_ref, w3_ref, w2_ref, fg_ref, o_ref, *, final_norm):
    x = x_ref[0]
    hb = (_rms(x) * (1.0 + mod_ref[0, 4:5, :]) + mod_ref[0, 3:4, :]).astype(BF16)
    a = jnp.dot(hb, w1_ref[0].astype(BF16), preferred_element_type=F32)
    g = jnp.dot(hb, w3_ref[0].astype(BF16), preferred_element_type=F32)
    s = (a * jax.nn.sigmoid(a) * g).astype(BF16)
    y = x + mod_ref[0, 5:6, :] * jnp.dot(s, w2_ref[0].astype(BF16), preferred_element_type=F32)
    if final_norm:
        y = _rms(y) * fg_ref[...]
        n_piece, steps, d = o_ref.shape[1:]
        o_ref[0] = jnp.swapaxes(y.reshape(steps, n_piece, d), 0, 1)
    else:
        o_ref[0] = y


def _ffn(x_all, mod, w1, w3, w2, layer, final_g, ctx_tiles, final_norm):
    b, t, d = x_all.shape
    tok = pl.BlockSpec((1, TM, d), lambda bb, i: (bb, i, 0))
    if final_norm:
        steps = TM // SUBLANES
        out_spec = pl.BlockSpec((1, SUBLANES, steps, d), lambda bb, i: (bb, 0, i, 0))
        out_shape = jax.ShapeDtypeStruct((b, SUBLANES, t // SUBLANES, d), F32)
    else:
        out_spec, out_shape = tok, jax.ShapeDtypeStruct((b, t, d), F32)
    out = pl.pallas_call(
        functools.partial(_ffn_kernel, final_norm=final_norm),
        grid=(b, t // TM),
        in_specs=[
            tok,
            pl.BlockSpec((1, N_MOD, d), lambda bb, i: (jnp.where(i < ctx_tiles, b, bb), 0, 0)),
            _layer_spec(w1.shape, layer), _layer_spec(w3.shape, layer), _layer_spec(w2.shape, layer),
            _const_spec(final_g.shape),
        ],
        out_specs=out_spec,
        out_shape=out_shape,
        compiler_params=_params(2),
        name="ffn",
    )(x_all, mod, w1, w3, w2, final_g)
    return out.reshape(b, t, d)


def _piece_order(n):
    r = np.arange(n, dtype=np.int64)
    return (r % SUBLANES) * (n // SUBLANES) + r // SUBLANES


def _rope_tables(n_ctx, n_lat):
    p = _piece_order(n_lat)
    rows = (p // GRID_W).astype(np.float32)
    cols = (p % GRID_W).astype(np.float32)
    n_freq = HEAD_DIM // 4
    inv = np.float32(ROPE_BASE) ** (-np.arange(n_freq, dtype=np.float32) / np.float32(n_freq))
    ang = np.concatenate([rows[:, None] * inv, cols[:, None] * inv], axis=-1)
    cos, sin = np.cos(ang), np.sin(ang)
    reps = LANES // HEAD_DIM
    cos_l = np.tile(np.concatenate([cos, cos], axis=-1), (1, reps))
    sin_l = np.tile(np.concatenate([-sin, sin], axis=-1), (1, reps))
    cos_t = np.concatenate([np.ones((n_ctx, LANES), np.float32), cos_l], axis=0)
    sin_t = np.concatenate([np.zeros((n_ctx, LANES), np.float32), sin_l], axis=0)
    return jnp.asarray(cos_t, F32), jnp.asarray(sin_t, F32)


def _dft_cos_sin(n, k, t):
    ang = ((k[:, None] * t[None, :]) % n) * (2.0 * math.pi / n)
    return np.cos(ang), np.sin(ang)


def _dft_parity_tables(n):
    steps, half = n // SUBLANES, SUBLANES // 2
    r_in = np.arange(n // 2, dtype=np.int64)
    step_in, piece_in = r_in // half, r_in % half
    t = piece_in * steps + step_in
    planes = []
    for par in range(2):
        step_out = np.arange(par, steps, 2, dtype=np.int64)
        k = (np.arange(SUBLANES, dtype=np.int64)[None, :] * steps + step_out[:, None]).reshape(-1)
        c, s = _dft_cos_sin(n, k, t)
        sign = np.where((step_in % 2 == 1) & (par == 1), -1.0, 1.0)
        planes.append(np.concatenate([c * sign, -s * sign], axis=1))
    return np.stack(planes)


def _dft_position_matrix(n):
    return jnp.asarray(_dft_parity_tables(n), F32).astype(BF16)


def kernel(x, c, ctx, c_ctx, ada_w, ada_b, w_in, rnn_conv_w, rnn_conv_b, rnn_wr, rnn_br, rnn_wi, rnn_bi,
           rnn_lambda, attn_lambda, w_branch, w_gate, b_gate, w_out, ffn_w1, ffn_w3, ffn_w2, final_g):
    b, n_lat, d = x.shape
    n_ctx = ctx.shape[1]
    depth = ada_w.shape[0]
    ctx_tiles = n_ctx // TM
    assert n_ctx == TM and n_lat % TM == 0

    cond = jnp.zeros((2 * SUBLANES, d), F32).at[:b].set(c).at[b].set(c_ctx)
    mod_all = _adaln(cond, ada_w, ada_b.reshape(depth, 1, N_MOD * d))

    cos_t, sin_t = _rope_tables(n_ctx, n_lat)
    ch = np.arange(FOURIER_GD, dtype=np.int64)
    cc, sc = _dft_cos_sin(FOURIER_GD, ch, ch)
    wf = jnp.asarray(np.concatenate([cc, sc], axis=1), F32).astype(BF16)
    a_lat = _dft_position_matrix(n_lat)
    a_ctx = _dft_position_matrix(n_ctx)
    fg = final_g.reshape(1, d)

    x_all = (ctx, x)
    lam_inits = tuple(0.8 - 0.6 * math.exp(-0.3 * l) for l in range(depth))
    lam_rows = _attn_lambda(attn_lambda, lam_inits)
    for l in range(depth):
        last = l == depth - 1
        mod = mod_all[l].reshape(2 * SUBLANES, N_MOD, d)
        outs = _inproj(x_all, mod, w_in, l, wf, cos_t, sin_t)
        if l == 0:
            x_all, outs = outs[0], outs[1:]
        h, xr, gr, xce, xse, xco, xso, qt0, qt1, k, vt = outs
        ya = _rglru(xr, gr, rnn_conv_w[l], rnn_conv_b[l].reshape(1, d), rnn_wr[l].astype(BF16), rnn_br[l],
                    rnn_wi[l].astype(BF16), rnn_bi[l], rnn_lambda[l], n_ctx)
        yb = _fourier(a_lat, a_ctx, xce, xse, xco, xso, n_ctx)
        yc = _attention(lam_rows[l], qt0, qt1, k, vt, n_ctx, lam_inits[l])
        tile0 = ctx_tiles if last else 0
        x_mid = _merge(x_all, mod, h, ya, yb, yc, w_branch, w_gate, b_gate[l].reshape(1, -1), w_out, l, tile0)
        x_all = _ffn(x_mid, mod, ffn_w1, ffn_w3, ffn_w2, l, fg, ctx_tiles - tile0, last)
    return x_all
```

```python
import functools
import math

import jax
import jax.numpy as jnp
import numpy as np
from jax import lax
from jax.experimental import pallas as pl
from jax.experimental.pallas import tpu as pltpu

F32 = jnp.float32
BF16 = jnp.bfloat16

EPS = 1e-6
N_MOD = 6
GRID_W = 64
RNN_BLOCK = 128
CONV_W = 4
LRU_C = 8.0
FOURIER_GD = 128
N_HEADS = 8
HEAD_DIM = 64
V_DIM = 2 * HEAD_DIM
ROPE_BASE = 10000.0

LANES = 128
SUBLANES = 8
TM = 256
SCAN_UNROLL = 8
KEY_CHUNK = 256
SCORE_LOOKAHEAD = 2
HEADS_PER_STEP = 4
BF16_SUBLANES = 16
VT_ROWS = V_DIM + BF16_SUBLANES
LOG2_E = math.log2(math.e)
VMEM_LIMIT = 56 * 1024 * 1024


def _params(n_axes):
    return pltpu.CompilerParams(
        dimension_semantics=("parallel",) * n_axes, vmem_limit_bytes=VMEM_LIMIT)


def _const_spec(shape):
    nd = len(shape)
    return pl.BlockSpec(shape, lambda *_: (0,) * nd, pipeline_mode=pl.Buffered(1))


def _layer_spec(shape, layer):
    nd = len(shape)
    return pl.BlockSpec((1,) + tuple(shape[1:]), lambda *_: (layer,) + (0,) * (nd - 1),
                        pipeline_mode=pl.Buffered(1))


def _rms(x):
    return x * lax.rsqrt(jnp.mean(x * x, axis=-1, keepdims=True) + EPS)


def _adaln_kernel(c_ref, w_ref, b_ref, o_ref):
    c = c_ref[...]
    s = (c * jax.nn.sigmoid(c)).astype(BF16)
    o_ref[0] = jnp.dot(s, w_ref[0].astype(BF16), preferred_element_type=F32) + b_ref[0]


def _adaln(cond, ada_w, ada_b):
    depth, d, n = ada_w.shape
    r = cond.shape[0]
    tn = 1536
    return pl.pallas_call(
        _adaln_kernel,
        grid=(depth, n // tn),
        in_specs=[
            pl.BlockSpec((r, d), lambda l, j: (0, 0)),
            pl.BlockSpec((1, d, tn), lambda l, j: (l, 0, j)),
            pl.BlockSpec((1, 1, tn), lambda l, j: (l, 0, j)),
        ],
        out_specs=pl.BlockSpec((1, r, tn), lambda l, j: (l, 0, j)),
        out_shape=jax.ShapeDtypeStruct((depth, r, n), F32),
        compiler_params=_params(2),
        name="adaln",
    )(cond, ada_w, ada_b)


def _inproj_kernel(*refs, from_time_order):
    if from_time_order:
        (ctx_ref, lat_ref, mod_ref, w_ref, wf_ref, cos_ref, sin_ref,
         x_ref, h_ref, xr_ref, gr_ref, xce_ref, xse_ref, xco_ref, xso_ref, qt0_ref, qt1_ref, k_ref, vt_ref) = refs
        blk = jnp.where(pl.program_id(1) == 0, ctx_ref[0], lat_ref[0])
        x = jnp.swapaxes(blk, 0, 1).reshape(x_ref.shape[1:])
        x_ref[0] = x
    else:
        (x_ref, mod_ref, w_ref, wf_ref, cos_ref, sin_ref,
         h_ref, xr_ref, gr_ref, xce_ref, xse_ref, xco_ref, xso_ref, qt0_ref, qt1_ref, k_ref, vt_ref) = refs
        x = x_ref[0]
    d = x.shape[-1]
    h = _rms(x) * (1.0 + mod_ref[0, 1:2, :]) + mod_ref[0, 0:1, :]
    hb = h.astype(BF16)
    h_ref[0] = hb

    def proj(j):
        return jnp.dot(hb, w_ref[0, :, j * d:(j + 1) * d].astype(BF16), preferred_element_type=F32)

    xr_ref[0] = proj(0)
    gr_ref[0] = proj(1)

    xf = proj(2).astype(BF16)
    steps = x.shape[0] // SUBLANES
    low = lax.broadcasted_iota(jnp.int32, (SUBLANES, 2 * FOURIER_GD), 0) < SUBLANES // 2
    for g in range(d // FOURIER_GD):
        sl = slice(g * FOURIER_GD, (g + 1) * FOURIER_GD)
        z = jnp.dot(xf[:, sl], wf_ref[...], preferred_element_type=F32)
        z3 = z.reshape(steps, SUBLANES, 2 * FOURIER_GD)
        zr = pltpu.roll(z3, SUBLANES // 2, 1)
        for folded, c_ref, s_ref in ((z3 + zr, xce_ref, xse_ref), (z3 - zr, xco_ref, xso_ref)):
            pair = folded.reshape(steps // 2, 2, SUBLANES, 2 * FOURIER_GD)
            kept = jnp.where(low, pair[:, 0], pair[:, 1]).reshape(x.shape[0] // 2, 2 * FOURIER_GD)
            c_ref[0, :, sl] = kept[:, :FOURIER_GD].astype(BF16)
            s_ref[0, :, sl] = kept[:, FOURIER_GD:].astype(BF16)

    cos = cos_ref[...]
    sin = sin_ref[...]
    lane = lax.broadcasted_iota(jnp.int32, cos.shape, 1)
    first_half = (lane % HEAD_DIM) < (HEAD_DIM // 2)

    def rope(th):
        swapped = jnp.where(first_half,
                            pltpu.roll(th, LANES - HEAD_DIM // 2, 1),
                            pltpu.roll(th, HEAD_DIM // 2, 1))
        return th * cos + swapped * sin

    q = proj(3)
    row = lax.broadcasted_iota(jnp.int32, (V_DIM, q.shape[0]), 0)
    zero_t = jnp.zeros((V_DIM, q.shape[0]), F32)
    for hh in range(d // V_DIM):
        sl = slice(hh * V_DIM, (hh + 1) * V_DIM)
        qt = (rope(q[:, sl]) * (HEAD_DIM ** -0.5 * LOG2_E)).T
        qt0_ref[0, sl, :] = jnp.where(row < HEAD_DIM, qt, zero_t).astype(BF16)
        qt1_ref[0, sl, :] = jnp.where(row >= HEAD_DIM, qt, zero_t).astype(BF16)

    k = proj(4)
    for hh in range(d // LANES):
        sl = slice(hh * LANES, (hh + 1) * LANES)
        k_ref[0, :, sl] = rope(k[:, sl]).astype(BF16)

    v = proj(5)
    ones = jnp.ones((VT_ROWS - V_DIM, v.shape[0]), BF16)
    for hh in range(d // V_DIM):
        vt_ref[0, hh * VT_ROWS:hh * VT_ROWS + V_DIM, :] = v[:, hh * V_DIM:(hh + 1) * V_DIM].T.astype(BF16)
        vt_ref[0, hh * VT_ROWS + V_DIM:(hh + 1) * VT_ROWS, :] = ones


def _inproj(x_src, mod, w_in, layer, wf, cos_t, sin_t):
    from_time_order = isinstance(x_src, tuple)
    d = mod.shape[-1]
    tok = pl.BlockSpec((1, TM, d), lambda bb, i: (bb, i, 0))
    if from_time_order:
        ctx, x = x_src
        b, t = x.shape[0], ctx.shape[1] + x.shape[1]
        steps = TM // SUBLANES
        as_pieces = lambda v: v.reshape(b, SUBLANES, v.shape[1] // SUBLANES, d)
        srcs = [as_pieces(ctx), as_pieces(x)]
        src_specs = [pl.BlockSpec((1, SUBLANES, steps, d), lambda bb, i: (bb, 0, 0, 0)),
                     pl.BlockSpec((1, SUBLANES, steps, d), lambda bb, i: (bb, 0, jnp.maximum(i - 1, 0), 0))]
    else:
        b, t, _ = x_src.shape
        srcs, src_specs = [x_src], [tok]
    nt = t // TM
    vt_rows = d // V_DIM * VT_ROWS
    tok_t = lambda rows: pl.BlockSpec((1, rows, TM), lambda bb, i: (bb, 0, i))
    f32_out = jax.ShapeDtypeStruct((b, t, d), F32)
    bf_out = jax.ShapeDtypeStruct((b, t, d), BF16)
    half_tok = pl.BlockSpec((1, TM // 2, d), lambda bb, i: (bb, i, 0))
    half_out = jax.ShapeDtypeStruct((b, t // 2, d), BF16)
    extra = from_time_order * 1
    return pl.pallas_call(
        functools.partial(_inproj_kernel, from_time_order=from_time_order),
        grid=(b, nt),
        in_specs=src_specs + [
            pl.BlockSpec((1, N_MOD, d), lambda bb, i: (jnp.where(i == 0, b, bb), 0, 0)),
            _layer_spec(w_in.shape, layer),
            _const_spec(wf.shape),
            pl.BlockSpec((TM, LANES), lambda bb, i: (i, 0)),
            pl.BlockSpec((TM, LANES), lambda bb, i: (i, 0)),
        ],
        out_specs=[tok] * (3 + extra) + [half_tok] * 4 + [tok_t(d), tok_t(d), tok, tok_t(vt_rows)],
        out_shape=[f32_out] * extra + [bf_out, f32_out, f32_out] + [half_out] * 4 + [
                                       jax.ShapeDtypeStruct((b, d, t), BF16),
                                       jax.ShapeDtypeStruct((b, d, t), BF16), bf_out,
                                       jax.ShapeDtypeStruct((b, vt_rows, t), BF16)],
        compiler_params=_params(2),
        name="inproj",
    )(*srcs, mod, w_in, wf, cos_t, sin_t)


def _log_sigmoid(x):
    return jnp.minimum(x, 0.0) - jnp.log(1.0 + jnp.exp(-jnp.abs(x)))


def _gelu_tanh(x):
    c0 = math.sqrt(2.0 / math.pi)
    half_x = 0.5 * x
    return half_x * jnp.tanh(x * (c0 + (c0 * 0.044715) * (x * x))) + half_x


def _rglru_kernel(xr_ref, gr_ref, cw_ref, cb_ref, wr_ref, br_ref, wi_ref, bi_ref, lam_ref, ya_ref,
                  xpad, a0, b0, a1, b1, h0, h1, a20, b20, a21, b21, *, n_ctx):
    t_all, c = xr_ref.shape[1], xr_ref.shape[2]
    n_piece = SUBLANES
    w = [cw_ref[k:k + 1, :] for k in range(CONV_W)]
    sub = lax.broadcasted_iota(jnp.int32, (n_piece, c), 0)
    zeros = jnp.zeros((n_piece, c), F32)

    def from_prev_piece(v):
        return jnp.where(sub >= 1, pltpu.roll(v, 1, 0), 0.0)

    def from_next_piece(v):
        return jnp.where(sub < n_piece - 1, pltpu.roll(v, n_piece - 1, 0), 0.0)

    def run_sequence(base, n, state_f, state_r):
        steps = n // n_piece
        x3 = xr_ref[0, base:base + n, :].reshape(steps, n_piece, c)
        xpad[2:steps + 2] = x3
        xpad[0] = from_prev_piece(x3[steps - 2])
        xpad[1] = from_prev_piece(x3[steps - 1])
        xpad[steps + 2] = from_next_piece(x3[0])
        u = xpad[0:steps] * w[0]
        u = u + xpad[1:steps + 1] * w[1]
        u = u + xpad[2:steps + 2] * w[2]
        u = u + xpad[3:steps + 3] * w[3]
        u = (u + cb_ref[...]).reshape(n, c)
        ub = u.astype(BF16)
        half_u = 0.5 * u
        totals = []
        for dd, (a_s, b_s, a2_s, b2_s) in enumerate(((a0, b0, a20, b20), (a1, b1, a21, b21))):
            tr = jnp.tanh(0.5 * (jnp.dot(ub, wr_ref[dd, 0], preferred_element_type=F32) + br_ref[dd:dd + 1, :]))
            ti = jnp.tanh(0.5 * (jnp.dot(ub, wi_ref[dd, 0], preferred_element_type=F32) + bi_ref[dd:dd + 1, :]))
            e2 = (0.5 * LRU_C * LOG2_E) * _log_sigmoid(lam_ref[dd:dd + 1, :])
            a = jnp.exp2(tr * e2 + e2)
            y = 1.0 - a * a
            root = jnp.where(y > 0.0, y * lax.rsqrt(y), 0.0)
            p = a.reshape(steps, n_piece, c)
            q = ((ti * half_u + half_u) * root).reshape(steps, n_piece, c)
            a_s[0:steps] = p
            b_s[0:steps] = q
            m = steps
            while m > 1:
                p = p.reshape(m // 2, 2, n_piece, c)
                q = q.reshape(m // 2, 2, n_piece, c)
                first, second = (0, 1) if dd == 0 else (1, 0)
                q = p[:, second] * q[:, first] + q[:, second]
                p = p[:, second] * p[:, first]
                m //= 2
                if m == steps // 2:
                    a2_s[0:m] = p
                    b2_s[0:m] = q
            totals.append((p[0], q[0]))
        (pf, hf), (pr, hr) = totals

        init_f = zeros
        for s in range(n_piece):
            init_f = jnp.where(sub == s, state_f, init_f)
            state_f = hf[s:s + 1, :] + pf[s:s + 1, :] * state_f
        init_r = zeros
        for s in reversed(range(n_piece)):
            init_r = jnp.where(sub == s, state_r, init_r)
            state_r = hr[s:s + 1, :] + pr[s:s + 1, :] * state_r

        def piece_states(k, carry):
            hf, hr = carry
            h0[2 * k] = a0[2 * k] * hf + b0[2 * k]
            hf = a20[k] * hf + b20[k]
            h0[2 * k + 1] = hf
            kr = steps // 2 - 1 - k
            h1[2 * kr + 1] = a1[2 * kr + 1] * hr + b1[2 * kr + 1]
            hr = a21[kr] * hr + b21[kr]
            h1[2 * kr] = hr
            return hf, hr

        lax.fori_loop(0, steps // 2, piece_states, (init_f, init_r), unroll=SCAN_UNROLL)
        h = (h0[0:steps] + h1[0:steps]).reshape(n, c)
        ya_ref[0, base:base + n, :] = (h * _gelu_tanh(gr_ref[0, base:base + n, :])).astype(BF16)
        return state_f, state_r

    zero_state = jnp.zeros((1, c), F32)
    state_f, state_r = run_sequence(0, n_ctx, zero_state, zero_state)
    run_sequence(n_ctx, t_all - n_ctx, state_f, state_r)


def _rglru(xr, gr, conv_w, conv_b, wr, br, wi, bi, lam, n_ctx):
    b, t, d = xr.shape
    c = RNN_BLOCK
    steps = max(n_ctx, t - n_ctx) // SUBLANES
    seq = pl.BlockSpec((1, t, c), lambda bb, g: (bb, 0, g))
    vec = lambda rows: pl.BlockSpec((rows, c), lambda bb, g: (0, g))
    mat = pl.BlockSpec((2, 1, c, c), lambda bb, g: (0, g, 0, 0))
    return pl.pallas_call(
        functools.partial(_rglru_kernel, n_ctx=n_ctx),
        grid=(b, d // c),
        in_specs=[seq, seq, vec(CONV_W), vec(1), mat, vec(2), mat, vec(2), vec(2)],
        out_specs=seq,
        out_shape=jax.ShapeDtypeStruct((b, t, d), BF16),
        scratch_shapes=([pltpu.VMEM((steps + CONV_W - 1, SUBLANES, c), F32)]
                        + [pltpu.VMEM((steps, SUBLANES, c), F32)] * 6
                        + [pltpu.VMEM((steps // 2, SUBLANES, c), F32)] * 4),
        compiler_params=_params(2),
        name="rglru",
    )(xr, gr, conv_w, conv_b, wr, br, wi, bi, lam)


def _fourier_kernel(al_ref, ac_ref, xce_ref, xse_ref, xco_ref, xso_ref, y_ref, *, n_ctx, scale_ctx, scale_lat):
    i = pl.program_id(1)
    half_ctx = n_ctx // 2
    half_lat = xce_ref.shape[1] - half_ctx
    rows, d = y_ref.shape[1], y_ref.shape[2]

    def mix(a_ref, lo, n2, scale):
        by_parity = []
        for par, (c_ref, s_ref) in enumerate(((xce_ref, xse_ref), (xco_ref, xso_ref))):
            y = jnp.dot(a_ref[par, :, :n2], c_ref[0, lo:lo + n2, :], preferred_element_type=F32)
            y = y + jnp.dot(a_ref[par, :, n2:], s_ref[0, lo:lo + n2, :], preferred_element_type=F32)
            by_parity.append((y * scale).reshape(rows // (2 * SUBLANES), SUBLANES, d))
        y_ref[0] = jnp.stack(by_parity, axis=1).reshape(rows, d).astype(BF16)

    @pl.when(i == 0)
    def _():
        mix(ac_ref, 0, half_ctx, scale_ctx)

    @pl.when(i > 0)
    def _():
        mix(al_ref, half_ctx, half_lat, scale_lat)


def _fourier(a_lat, a_ctx, xce, xse, xco, xso, n_ctx):
    b, t2, d = xce.shape
    t = 2 * t2
    n_lat = t - n_ctx
    assert n_ctx == TM
    slab = pl.BlockSpec((1, t2, d), lambda bb, i: (bb, 0, 0))
    return pl.pallas_call(
        functools.partial(_fourier_kernel, n_ctx=n_ctx,
                          scale_ctx=(n_ctx * FOURIER_GD) ** -0.5,
                          scale_lat=(n_lat * FOURIER_GD) ** -0.5),
        grid=(b, t // TM),
        in_specs=[
            pl.BlockSpec((2, TM // 2, n_lat), lambda bb, i: (0, jnp.maximum(i - 1, 0), 0)),
            _const_spec(a_ctx.shape),
            slab, slab, slab, slab,
        ],
        out_specs=pl.BlockSpec((1, TM, d), lambda bb, i: (bb, i, 0)),
        out_shape=jax.ShapeDtypeStruct((b, t, d), BF16),
        compiler_params=_params(2),
        name="fourier",
    )(a_lat, a_ctx, xce, xse, xco, xso)


def _attn_lambda_kernel(lv_ref, o_ref, *, lam_inits):
    for l, lam_init in enumerate(lam_inits):
        lv = lv_ref[l]
        lam = (jnp.exp(jnp.sum(lv[0:1] * lv[1:2], axis=1, keepdims=True))
               - jnp.exp(jnp.sum(lv[2:3] * lv[3:4], axis=1, keepdims=True)) + lam_init)
        o_ref[l] = jnp.broadcast_to(lam, o_ref.shape[1:])


def _attn_lambda(attn_lambda, lam_inits):
    depth = attn_lambda.shape[0]
    return pl.pallas_call(
        functools.partial(_attn_lambda_kernel, lam_inits=lam_inits),
        out_shape=jax.ShapeDtypeStruct((depth, 1, TM), F32),
        name="attn_lambda",
    )(attn_lambda)


def _attn_kernel(lam_ref, qt0_ref, qt1_ref, k_ref, vt_ref, o_ref, *, n_ctx, lam_init):
    i = pl.program_id(2)
    streams = [(hd, qt_ref[0, hd * V_DIM:(hd + 1) * V_DIM, :])
               for hd in range(HEADS_PER_STEP) for qt_ref in (qt0_ref, qt1_ref)]

    def attend(n_keys):
        m = [None] * len(streams)
        u = [None] * len(streams)

        def scores(c0):
            return [jnp.dot(k_ref[0, c0:c0 + KEY_CHUNK, hd * V_DIM:(hd + 1) * V_DIM], qt,
                            preferred_element_type=F32) for hd, qt in streams]

        starts = list(range(0, n_keys, KEY_CHUNK))
        pending = [scores(c0) for c0 in starts[:SCORE_LOOKAHEAD]]
        for n, c0 in enumerate(starts):
            if n + SCORE_LOOKAHEAD < len(starts):
                pending.append(scores(starts[n + SCORE_LOOKAHEAD]))
            sts = pending.pop(0)
            for j, (hd, _) in enumerate(streams):
                m_chunk = jnp.max(sts[j], axis=0, keepdims=True)
                m_new = m_chunk if m[j] is None else jnp.maximum(m[j], m_chunk)
                pt = jnp.exp2(sts[j] - m_new).astype(BF16)
                part = jnp.dot(vt_ref[0, hd * VT_ROWS:(hd + 1) * VT_ROWS, c0:c0 + KEY_CHUNK], pt,
                               preferred_element_type=F32)
                u[j] = part if u[j] is None else u[j] * jnp.exp2(m[j] - m_new) + part
                m[j] = m_new
        for hd in range(HEADS_PER_STEP):
            u0, u1 = u[2 * hd], u[2 * hd + 1]
            ot = u0[:V_DIM] * (1.0 / u0[V_DIM:V_DIM + 1]) - u1[:V_DIM] * (lam_ref[...] / u1[V_DIM:V_DIM + 1])
            ot = ot * lax.rsqrt(jnp.mean(ot * ot, axis=0, keepdims=True) + EPS) * (1.0 - lam_init)
            o_ref[0, :, hd * V_DIM:(hd + 1) * V_DIM] = ot.T.astype(BF16)

    @pl.when(i == 0)
    def _():
        attend(n_ctx)

    @pl.when(i > 0)
    def _():
        attend(k_ref.shape[1])


def _attention(lam_row, qt0, qt1, k, vt, n_ctx, lam_init):
    b, t, d = k.shape
    assert n_ctx == TM
    hps = HEADS_PER_STEP
    qt = pl.BlockSpec((1, hps * V_DIM, TM), lambda bb, hh, i: (bb, hh, i))
    keys = pl.BlockSpec((1, t, hps * V_DIM), lambda bb, hh, i: (bb, 0, hh))
    vals = pl.BlockSpec((1, hps * VT_ROWS, t), lambda bb, hh, i: (bb, hh, 0))
    return pl.pallas_call(
        functools.partial(_attn_kernel, n_ctx=n_ctx, lam_init=lam_init),
        grid=(b, d // (hps * V_DIM), t // TM),
        in_specs=[pl.BlockSpec(lam_row.shape, lambda bb, hh, i: (0, 0)), qt, qt, keys, vals],
        out_specs=pl.BlockSpec((1, TM, hps * V_DIM), lambda bb, hh, i: (bb, i, hh)),
        out_shape=jax.ShapeDtypeStruct((b, t, d), BF16),
        compiler_params=_params(3),
        name="diffattn",
    )(lam_row, qt0, qt1, k, vt)


def _merge_kernel(x_ref, mod_ref, h_ref, ya_ref, yb_ref, yc_ref, wb_ref, wg_ref, bg_ref, wo_ref, o_ref):
    d = x_ref.shape[-1]
    g = jax.nn.sigmoid(jnp.dot(h_ref[0], wg_ref[0].astype(BF16), preferred_element_type=F32) + bg_ref[...])
    m = g[:, :d] * jnp.dot(ya_ref[0], wb_ref[0, :d, :].astype(BF16), preferred_element_type=F32)
    m = m + g[:, d:2 * d] * jnp.dot(yb_ref[0], wb_ref[0, d:2 * d, :].astype(BF16), preferred_element_type=F32)
    m = m + g[:, 2 * d:] * jnp.dot(yc_ref[0], wb_ref[0, 2 * d:, :].astype(BF16), preferred_element_type=F32)
    out = jnp.dot(m.astype(BF16), wo_ref[0].astype(BF16), preferred_element_type=F32)
    o_ref[0] = x_ref[0] + mod_ref[0, 2:3, :] * out


def _merge(x_all, mod, h, ya, yb, yc, w_branch, w_gate, b_gate, w_out, layer, tile0):
    b, t, d = x_all.shape
    nt = t // TM - tile0
    tok = pl.BlockSpec((1, TM, d), lambda bb, i: (bb, i + tile0, 0))
    return pl.pallas_call(
        _merge_kernel,
        grid=(b, nt),
        in_specs=[
            tok,
            pl.BlockSpec((1, N_MOD, d), lambda bb, i: (jnp.where(i + tile0 == 0, b, bb), 0, 0)),
            tok, tok, tok, tok,
            _layer_spec(w_branch.shape, layer), _layer_spec(w_gate.shape, layer), _const_spec(b_gate.shape),
            _layer_spec(w_out.shape, layer),
        ],
        out_specs=pl.BlockSpec((1, TM, d), lambda bb, i: (bb, i, 0)),
        out_shape=jax.ShapeDtypeStruct((b, nt * TM, d), F32),
        compiler_params=_params(2),
        name="merge",
    )(x_all, mod, h, ya, yb, yc, w_branch, w_gate, b_gate, w_out)


def _ffn_kernel(x_ref, mod_ref, w1_ref, w3_ref, w2_ref, fg_ref, o_ref, *, final_norm):
    x = x_ref[0]
    hb = (_rms(x) * (1.0 + mod_ref[0, 4:5, :]) + mod_ref[0, 3:4, :]).astype(BF16)
    a = jnp.dot(hb, w1_ref[0].astype(BF16), preferred_element_type=F32)
    g = jnp.dot(hb, w3_ref[0].astype(BF16), preferred_element_type=F32)
    s = (a * jax.nn.sigmoid(a) * g).astype(BF16)
    y = x + mod_ref[0, 5:6, :] * jnp.dot(s, w2_ref[0].astype(BF16), preferred_element_type=F32)
    if final_norm:
        y = _rms(y) * fg_ref[...]
        n_piece, steps, d = o_ref.shape[1:]
        o_ref[0] = jnp.swapaxes(y.reshape(steps, n_piece, d), 0, 1)
    else:
        o_ref[0] = y


def _ffn(x_all, mod, w1, w3, w2, layer, final_g, ctx_tiles, final_norm):
    b, t, d = x_all.shape
    tok = pl.BlockSpec((1, TM, d), lambda bb, i: (bb, i, 0))
    if final_norm:
        steps = TM // SUBLANES
        out_spec = pl.BlockSpec((1, SUBLANES, steps, d), lambda bb, i: (bb, 0, i, 0))
        out_shape = jax.ShapeDtypeStruct((b, SUBLANES, t // SUBLANES, d), F32)
    else:
        out_spec, out_shape = tok, jax.ShapeDtypeStruct((b, t, d), F32)
    out = pl.pallas_call(
        functools.partial(_ffn_kernel, final_norm=final_norm),
        grid=(b, t // TM),
        in_specs=[
            tok,
            pl.BlockSpec((1, N_MOD, d), lambda bb, i: (jnp.where(i < ctx_tiles, b, bb), 0, 0)),
            _layer_spec(w1.shape, layer), _layer_spec(w3.shape, layer), _layer_spec(w2.shape, layer),
            _const_spec(final_g.shape),
        ],
        out_specs=out_spec,
        out_shape=out_shape,
        compiler_params=_params(2),
        name="ffn",
    )(x_all, mod, w1, w3, w2, final_g)
    return out.reshape(b, t, d)


def _piece_order(n):
    r = np.arange(n, dtype=np.int64)
    return (r % SUBLANES) * (n // SUBLANES) + r // SUBLANES


def _rope_tables(n_ctx, n_lat):
    p = _piece_order(n_lat)
    rows = (p // GRID_W).astype(np.float32)
    cols = (p % GRID_W).astype(np.float32)
    n_freq = HEAD_DIM // 4
    inv = np.float32(ROPE_BASE) ** (-np.arange(n_freq, dtype=np.float32) / np.float32(n_freq))
    ang = np.concatenate([rows[:, None] * inv, cols[:, None] * inv], axis=-1)
    cos, sin = np.cos(ang), np.sin(ang)
    reps = LANES // HEAD_DIM
    cos_l = np.tile(np.concatenate([cos, cos], axis=-1), (1, reps))
    sin_l = np.tile(np.concatenate([-sin, sin], axis=-1), (1, reps))
    cos_t = np.concatenate([np.ones((n_ctx, LANES), np.float32), cos_l], axis=0)
    sin_t = np.concatenate([np.zeros((n_ctx, LANES), np.float32), sin_l], axis=0)
    return jnp.asarray(cos_t, F32), jnp.asarray(sin_t, F32)


def _dft_cos_sin(n, k, t):
    ang = ((k[:, None] * t[None, :]) % n) * (2.0 * math.pi / n)
    return np.cos(ang), np.sin(ang)


def _dft_parity_tables(n):
    steps, half = n // SUBLANES, SUBLANES // 2
    r_in = np.arange(n // 2, dtype=np.int64)
    step_in, piece_in = r_in // half, r_in % half
    t = piece_in * steps + step_in
    planes = []
    for par in range(2):
        step_out = np.arange(par, steps, 2, dtype=np.int64)
        k = (np.arange(SUBLANES, dtype=np.int64)[None, :] * steps + step_out[:, None]).reshape(-1)
        c, s = _dft_cos_sin(n, k, t)
        sign = np.where((step_in % 2 == 1) & (par == 1), -1.0, 1.0)
        planes.append(np.concatenate([c * sign, -s * sign], axis=1))
    return np.stack(planes)


def _dft_position_matrix(n):
    return jnp.asarray(_dft_parity_tables(n), F32).astype(BF16)


def kernel(x, c, ctx, c_ctx, ada_w, ada_b, w_in, rnn_conv_w, rnn_conv_b, rnn_wr, rnn_br, rnn_wi, rnn_bi,
           rnn_lambda, attn_lambda, w_branch, w_gate, b_gate, w_out, ffn_w1, ffn_w3, ffn_w2, final_g):
    b, n_lat, d = x.shape
    n_ctx = ctx.shape[1]
    depth = ada_w.shape[0]
    ctx_tiles = n_ctx // TM
    assert n_ctx == TM and n_lat % TM == 0

    cond = jnp.zeros((2 * SUBLANES, d), F32).at[:b].set(c).at[b].set(c_ctx)
    mod_all = _adaln(cond, ada_w, ada_b.reshape(depth, 1, N_MOD * d))

    cos_t, sin_t = _rope_tables(n_ctx, n_lat)
    ch = np.arange(FOURIER_GD, dtype=np.int64)
    cc, sc = _dft_cos_sin(FOURIER_GD, ch, ch)
    wf = jnp.asarray(np.concatenate([cc, sc], axis=1), F32).astype(BF16)
    a_lat = _dft_position_matrix(n_lat)
    a_ctx = _dft_position_matrix(n_ctx)
    fg = final_g.reshape(1, d)

    x_all = (ctx, x)
    lam_inits = tuple(0.8 - 0.6 * math.exp(-0.3 * l) for l in range(depth))
    lam_rows = _attn_lambda(attn_lambda, lam_inits)
    for l in range(depth):
        last = l == depth - 1
        mod = mod_all[l].reshape(2 * SUBLANES, N_MOD, d)
        outs = _inproj(x_all, mod, w_in, l, wf, cos_t, sin_t)
        if l == 0:
            x_all, outs = outs[0], outs[1:]
        h, xr, gr, xce, xse, xco, xso, qt0, qt1, k, vt = outs
        ya = _rglru(xr, gr, rnn_conv_w[l], rnn_conv_b[l].reshape(1, d), rnn_wr[l].astype(BF16), rnn_br[l],
                    rnn_wi[l].astype(BF16), rnn_bi[l], rnn_lambda[l], n_ctx)
        yb = _fourier(a_lat, a_ctx, xce, xse, xco, xso, n_ctx)
        yc = _attention(lam_rows[l], qt0, qt1, k, vt, n_ctx, lam_inits[l])
        tile0 = ctx_tiles if last else 0
        x_mid = _merge(x_all, mod, h, ya, yb, yc, w_branch, w_gate, b_gate[l].reshape(1, -1), w_out, l, tile0)
        x_all = _ffn(x_mid, mod, ffn_w1, ffn_w3, ffn_w2, l, fg, ctx_tiles - tile0, last)
    return x_all
```

```python
import functools
import math

import jax
import jax.numpy as jnp
import numpy as np
from jax import lax
from jax.experimental import pallas as pl
from jax.experimental.pallas import tpu as pltpu

F32 = jnp.float32
BF16 = jnp.bfloat16

EPS = 1e-6
N_MOD = 6
GRID_W = 64
RNN_BLOCK = 128
CONV_W = 4
LRU_C = 8.0
FOURIER_GD = 128
N_HEADS = 8
HEAD_DIM = 64
V_DIM = 2 * HEAD_DIM
ROPE_BASE = 10000.0

LANES = 128
SUBLANES = 8
TM = 256
SCAN_UNROLL = 8
KEY_CHUNK = 256
SCORE_LOOKAHEAD = 1
HEADS_PER_STEP = 4
BF16_SUBLANES = 16
VT_ROWS = V_DIM + BF16_SUBLANES
LOG2_E = math.log2(math.e)
VMEM_LIMIT = 56 * 1024 * 1024


def _params(n_axes):
    return pltpu.CompilerParams(
        dimension_semantics=("parallel",) * n_axes, vmem_limit_bytes=VMEM_LIMIT)


def _const_spec(shape):
    nd = len(shape)
    return pl.BlockSpec(shape, lambda *_: (0,) * nd, pipeline_mode=pl.Buffered(1))


def _layer_spec(shape, layer):
    nd = len(shape)
    return pl.BlockSpec((1,) + tuple(shape[1:]), lambda *_: (layer,) + (0,) * (nd - 1),
                        pipeline_mode=pl.Buffered(1))


def _rms(x):
    return x * lax.rsqrt(jnp.mean(x * x, axis=-1, keepdims=True) + EPS)


def _adaln_kernel(c_ref, w_ref, b_ref, o_ref):
    c = c_ref[...]
    s = (c * jax.nn.sigmoid(c)).astype(BF16)
    o_ref[0] = jnp.dot(s, w_ref[0].astype(BF16), preferred_element_type=F32) + b_ref[0]


def _adaln(cond, ada_w, ada_b):
    depth, d, n = ada_w.shape
    r = cond.shape[0]
    tn = 1536
    return pl.pallas_call(
        _adaln_kernel,
        grid=(depth, n // tn),
        in_specs=[
            pl.BlockSpec((r, d), lambda l, j: (0, 0)),
            pl.BlockSpec((1, d, tn), lambda l, j: (l, 0, j)),
            pl.BlockSpec((1, 1, tn), lambda l, j: (l, 0, j)),
        ],
        out_specs=pl.BlockSpec((1, r, tn), lambda l, j: (l, 0, j)),
        out_shape=jax.ShapeDtypeStruct((depth, r, n), F32),
        compiler_params=_params(2),
        name="adaln",
    )(cond, ada_w, ada_b)


def _inproj_kernel(*refs, from_time_order):
    if from_time_order:
        (ctx_ref, lat_ref, mod_ref, w_ref, wf_ref, cos_ref, sin_ref,
         x_ref, h_ref, xr_ref, gr_ref, xce_ref, xse_ref, xco_ref, xso_ref, qt0_ref, qt1_ref, k_ref, vt_ref) = refs
        blk = jnp.where(pl.program_id(1) == 0, ctx_ref[0], lat_ref[0])
        x = jnp.swapaxes(blk, 0, 1).reshape(x_ref.shape[1:])
        x_ref[0] = x
    else:
        (x_ref, mod_ref, w_ref, wf_ref, cos_ref, sin_ref,
         h_ref, xr_ref, gr_ref, xce_ref, xse_ref, xco_ref, xso_ref, qt0_ref, qt1_ref, k_ref, vt_ref) = refs
        x = x_ref[0]
    d = x.shape[-1]
    h = _rms(x) * (1.0 + mod_ref[0, 1:2, :]) + mod_ref[0, 0:1, :]
    hb = h.astype(BF16)
    h_ref[0] = hb

    def proj(j):
        return jnp.dot(hb, w_ref[0, :, j * d:(j + 1) * d].astype(BF16), preferred_element_type=F32)

    xr_ref[0] = proj(0)
    gr_ref[0] = proj(1)

    xf = proj(2).astype(BF16)
    steps = x.shape[0] // SUBLANES
    low = lax.broadcasted_iota(jnp.int32, (SUBLANES, 2 * FOURIER_GD), 0) < SUBLANES // 2
    for g in range(d // FOURIER_GD):
        sl = slice(g * FOURIER_GD, (g + 1) * FOURIER_GD)
        z = jnp.dot(xf[:, sl], wf_ref[...], preferred_element_type=F32)
        z3 = z.reshape(steps, SUBLANES, 2 * FOURIER_GD)
        zr = pltpu.roll(z3, SUBLANES // 2, 1)
        for folded, c_ref, s_ref in ((z3 + zr, xce_ref, xse_ref), (z3 - zr, xco_ref, xso_ref)):
            pair = folded.reshape(steps // 2, 2, SUBLANES, 2 * FOURIER_GD)
            kept = jnp.where(low, pair[:, 0], pair[:, 1]).reshape(x.shape[0] // 2, 2 * FOURIER_GD)
            c_ref[0, :, sl] = kept[:, :FOURIER_GD].astype(BF16)
            s_ref[0, :, sl] = kept[:, FOURIER_GD:].astype(BF16)

    cos = cos_ref[...]
    sin = sin_ref[...]
    lane = lax.broadcasted_iota(jnp.int32, cos.shape, 1)
    first_half = (lane % HEAD_DIM) < (HEAD_DIM // 2)

    def rope(th):
        swapped = jnp.where(first_half,
                            pltpu.roll(th, LANES - HEAD_DIM // 2, 1),
                            pltpu.roll(th, HEAD_DIM // 2, 1))
        return th * cos + swapped * sin

    q = proj(3)
    row = lax.broadcasted_iota(jnp.int32, (V_DIM, q.shape[0]), 0)
    zero_t = jnp.zeros((V_DIM, q.shape[0]), F32)
    for hh in range(d // V_DIM):
        sl = slice(hh * V_DIM, (hh + 1) * V_DIM)
        qt = (rope(q[:, sl]) * (HEAD_DIM ** -0.5 * LOG2_E)).T
        qt0_ref[0, sl, :] = jnp.where(row < HEAD_DIM, qt, zero_t).astype(BF16)
        qt1_ref[0, sl, :] = jnp.where(row >= HEAD_DIM, qt, zero_t).astype(BF16)

    k = proj(4)
    for hh in range(d // LANES):
        sl = slice(hh * LANES, (hh + 1) * LANES)
        k_ref[0, :, sl] = rope(k[:, sl]).astype(BF16)

    v = proj(5)
    ones = jnp.ones((VT_ROWS - V_DIM, v.shape[0]), BF16)
    for hh in range(d // V_DIM):
        vt_ref[0, hh * VT_ROWS:hh * VT_ROWS + V_DIM, :] = v[:, hh * V_DIM:(hh + 1) * V_DIM].T.astype(BF16)
        vt_ref[0, hh * VT_ROWS + V_DIM:(hh + 1) * VT_ROWS, :] = ones


def _inproj(x_src, mod, w_in, layer, wf, cos_t, sin_t):
    from_time_order = isinstance(x_src, tuple)
    d = mod.shape[-1]
    tok = pl.BlockSpec((1, TM, d), lambda bb, i: (bb, i, 0))
    if from_time_order:
        ctx, x = x_src
        b, t = x.shape[0], ctx.shape[1] + x.shape[1]
        steps = TM // SUBLANES
        as_pieces = lambda v: v.reshape(b, SUBLANES, v.shape[1] // SUBLANES, d)
        srcs = [as_pieces(ctx), as_pieces(x)]
        src_specs = [pl.BlockSpec((1, SUBLANES, steps, d), lambda bb, i: (bb, 0, 0, 0)),
                     pl.BlockSpec((1, SUBLANES, steps, d), lambda bb, i: (bb, 0, jnp.maximum(i - 1, 0), 0))]
    else:
        b, t, _ = x_src.shape
        srcs, src_specs = [x_src], [tok]
    nt = t // TM
    vt_rows = d // V_DIM * VT_ROWS
    tok_t = lambda rows: pl.BlockSpec((1, rows, TM), lambda bb, i: (bb, 0, i))
    f32_out = jax.ShapeDtypeStruct((b, t, d), F32)
    bf_out = jax.ShapeDtypeStruct((b, t, d), BF16)
    half_tok = pl.BlockSpec((1, TM // 2, d), lambda bb, i: (bb, i, 0))
    half_out = jax.ShapeDtypeStruct((b, t // 2, d), BF16)
    extra = from_time_order * 1
    return pl.pallas_call(
        functools.partial(_inproj_kernel, from_time_order=from_time_order),
        grid=(b, nt),
        in_specs=src_specs + [
            pl.BlockSpec((1, N_MOD, d), lambda bb, i: (jnp.where(i == 0, b, bb), 0, 0)),
            _layer_spec(w_in.shape, layer),
            _const_spec(wf.shape),
            pl.BlockSpec((TM, LANES), lambda bb, i: (i, 0)),
            pl.BlockSpec((TM, LANES), lambda bb, i: (i, 0)),
        ],
        out_specs=[tok] * (3 + extra) + [half_tok] * 4 + [tok_t(d), tok_t(d), tok, tok_t(vt_rows)],
        out_shape=[f32_out] * extra + [bf_out, f32_out, f32_out] + [half_out] * 4 + [
                                       jax.ShapeDtypeStruct((b, d, t), BF16),
                                       jax.ShapeDtypeStruct((b, d, t), BF16), bf_out,
                                       jax.ShapeDtypeStruct((b, vt_rows, t), BF16)],
        compiler_params=_params(2),
        name="inproj",
    )(*srcs, mod, w_in, wf, cos_t, sin_t)


def _log_sigmoid(x):
    return jnp.minimum(x, 0.0) - jnp.log(1.0 + jnp.exp(-jnp.abs(x)))


def _gelu_tanh(x):
    c0 = math.sqrt(2.0 / math.pi)
    half_x = 0.5 * x
    return half_x * jnp.tanh(x * (c0 + (c0 * 0.044715) * (x * x))) + half_x


def _rglru_kernel(xr_ref, gr_ref, cw_ref, cb_ref, wr_ref, br_ref, wi_ref, bi_ref, lam_ref, ya_ref,
                  xpad, a0, b0, a1, b1, h0, h1, a20, b20, a21, b21, *, n_ctx):
    t_all, c = xr_ref.shape[1], xr_ref.shape[2]
    n_piece = SUBLANES
    w = [cw_ref[k:k + 1, :] for k in range(CONV_W)]
    sub = lax.broadcasted_iota(jnp.int32, (n_piece, c), 0)
    zeros = jnp.zeros((n_piece, c), F32)

    def from_prev_piece(v):
        return jnp.where(sub >= 1, pltpu.roll(v, 1, 0), 0.0)

    def from_next_piece(v):
        return jnp.where(sub < n_piece - 1, pltpu.roll(v, n_piece - 1, 0), 0.0)

    def run_sequence(base, n, state_f, state_r):
        steps = n // n_piece
        x3 = xr_ref[0, base:base + n, :].reshape(steps, n_piece, c)
        xpad[2:steps + 2] = x3
        xpad[0] = from_prev_piece(x3[steps - 2])
        xpad[1] = from_prev_piece(x3[steps - 1])
        xpad[steps + 2] = from_next_piece(x3[0])
        u = xpad[0:steps] * w[0]
        u = u + xpad[1:steps + 1] * w[1]
        u = u + xpad[2:steps + 2] * w[2]
        u = u + xpad[3:steps + 3] * w[3]
        u = (u + cb_ref[...]).reshape(n, c)
        ub = u.astype(BF16)
        half_u = 0.5 * u
        totals = []
        for dd, (a_s, b_s, a2_s, b2_s) in enumerate(((a0, b0, a20, b20), (a1, b1, a21, b21))):
            tr = jnp.tanh(0.5 * (jnp.dot(ub, wr_ref[dd, 0], preferred_element_type=F32) + br_ref[dd:dd + 1, :]))
            ti = jnp.tanh(0.5 * (jnp.dot(ub, wi_ref[dd, 0], preferred_element_type=F32) + bi_ref[dd:dd + 1, :]))
            e2 = (0.5 * LRU_C * LOG2_E) * _log_sigmoid(lam_ref[dd:dd + 1, :])
            a = jnp.exp2(tr * e2 + e2)
            y = 1.0 - a * a
            root = jnp.where(y > 0.0, y * lax.rsqrt(y), 0.0)
            p = a.reshape(steps, n_piece, c)
            q = ((ti * half_u + half_u) * root).reshape(steps, n_piece, c)
            a_s[0:steps] = p
            b_s[0:steps] = q
            m = steps
            while m > 1:
                p = p.reshape(m // 2, 2, n_piece, c)
                q = q.reshape(m // 2, 2, n_piece, c)
                first, second = (0, 1) if dd == 0 else (1, 0)
                q = p[:, second] * q[:, first] + q[:, second]
                p = p[:, second] * p[:, first]
                m //= 2
                if m == steps // 2:
                    a2_s[0:m] = p
                    b2_s[0:m] = q
            totals.append((p[0], q[0]))
        (pf, hf), (pr, hr) = totals

        init_f = zeros
        for s in range(n_piece):
            init_f = jnp.where(sub == s, state_f, init_f)
            state_f = hf[s:s + 1, :] + pf[s:s + 1, :] * state_f
        init_r = zeros
        for s in reversed(range(n_piece)):
            init_r = jnp.where(sub == s, state_r, init_r)
            state_r = hr[s:s + 1, :] + pr[s:s + 1, :] * state_r

        def piece_states(k, carry):
            hf, hr = carry
            h0[2 * k] = a0[2 * k] * hf + b0[2 * k]
            hf = a20[k] * hf + b20[k]
            h0[2 * k + 1] = hf
            kr = steps // 2 - 1 - k
            h1[2 * kr + 1] = a1[2 * kr + 1] * hr + b1[2 * kr + 1]
            hr = a21[kr] * hr + b21[kr]
            h1[2 * kr] = hr
            return hf, hr

        lax.fori_loop(0, steps // 2, piece_states, (init_f, init_r), unroll=SCAN_UNROLL)
        h = (h0[0:steps] + h1[0:steps]).reshape(n, c)
        ya_ref[0, base:base + n, :] = (h * _gelu_tanh(gr_ref[0, base:base + n, :])).astype(BF16)
        return state_f, state_r

    zero_state = jnp.zeros((1, c), F32)
    state_f, state_r = run_sequence(0, n_ctx, zero_state, zero_state)
    run_sequence(n_ctx, t_all - n_ctx, state_f, state_r)


def _rglru(xr, gr, conv_w, conv_b, wr, br, wi, bi, lam, n_ctx):
    b, t, d = xr.shape
    c = RNN_BLOCK
    steps = max(n_ctx, t - n_ctx) // SUBLANES
    seq = pl.BlockSpec((1, t, c), lambda bb, g: (bb, 0, g))
    vec = lambda rows: pl.BlockSpec((rows, c), lambda bb, g: (0, g))
    mat = pl.BlockSpec((2, 1, c, c), lambda bb, g: (0, g, 0, 0))
    return pl.pallas_call(
        functools.partial(_rglru_kernel, n_ctx=n_ctx),
        grid=(b, d // c),
        in_specs=[seq, seq, vec(CONV_W), vec(1), mat, vec(2), mat, vec(2), vec(2)],
        out_specs=seq,
        out_shape=jax.ShapeDtypeStruct((b, t, d), BF16),
        scratch_shapes=([pltpu.VMEM((steps + CONV_W - 1, SUBLANES, c), F32)]
                        + [pltpu.VMEM((steps, SUBLANES, c), F32)] * 6
                        + [pltpu.VMEM((steps // 2, SUBLANES, c), F32)] * 4),
        compiler_params=_params(2),
        name="rglru",
    )(xr, gr, conv_w, conv_b, wr, br, wi, bi, lam)


def _fourier_kernel(al_ref, ac_ref, xce_ref, xse_ref, xco_ref, xso_ref, y_ref, *, n_ctx, scale_ctx, scale_lat):
    i = pl.program_id(1)
    half_ctx = n_ctx // 2
    half_lat = xce_ref.shape[1] - half_ctx
    rows, d = y_ref.shape[1], y_ref.shape[2]

    def mix(a_ref, lo, n2, scale):
        by_parity = []
        for par, (c_ref, s_ref) in enumerate(((xce_ref, xse_ref), (xco_ref, xso_ref))):
            y = jnp.dot(a_ref[par, :, :n2], c_ref[0, lo:lo + n2, :], preferred_element_type=F32)
            y = y + jnp.dot(a_ref[par, :, n2:], s_ref[0, lo:lo + n2, :], preferred_element_type=F32)
            by_parity.append((y * scale).reshape(rows // (2 * SUBLANES), SUBLANES, d))
        y_ref[0] = jnp.stack(by_parity, axis=1).reshape(rows, d).astype(BF16)

    @pl.when(i == 0)
    def _():
        mix(ac_ref, 0, half_ctx, scale_ctx)

    @pl.when(i > 0)
    def _():
        mix(al_ref, half_ctx, half_lat, scale_lat)


def _fourier(a_lat, a_ctx, xce, xse, xco, xso, n_ctx):
    b, t2, d = xce.shape
    t = 2 * t2
    n_lat = t - n_ctx
    assert n_ctx == TM
    slab = pl.BlockSpec((1, t2, d), lambda bb, i: (bb, 0, 0))
    return pl.pallas_call(
        functools.partial(_fourier_kernel, n_ctx=n_ctx,
                          scale_ctx=(n_ctx * FOURIER_GD) ** -0.5,
                          scale_lat=(n_lat * FOURIER_GD) ** -0.5),
        grid=(b, t // TM),
        in_specs=[
            pl.BlockSpec((2, TM // 2, n_lat), lambda bb, i: (0, jnp.maximum(i - 1, 0), 0)),
            _const_spec(a_ctx.shape),
            slab, slab, slab, slab,
        ],
        out_specs=pl.BlockSpec((1, TM, d), lambda bb, i: (bb, i, 0)),
        out_shape=jax.ShapeDtypeStruct((b, t, d), BF16),
        compiler_params=_params(2),
        name="fourier",
    )(a_lat, a_ctx, xce, xse, xco, xso)


def _attn_lambda_kernel(lv_ref, o_ref, *, lam_inits):
    for l, lam_init in enumerate(lam_inits):
        lv = lv_ref[l]
        lam = (jnp.exp(jnp.sum(lv[0:1] * lv[1:2], axis=1, keepdims=True))
               - jnp.exp(jnp.sum(lv[2:3] * lv[3:4], axis=1, keepdims=True)) + lam_init)
        o_ref[l] = jnp.broadcast_to(lam, o_ref.shape[1:])


def _attn_lambda(attn_lambda, lam_inits):
    depth = attn_lambda.shape[0]
    return pl.pallas_call(
        functools.partial(_attn_lambda_kernel, lam_inits=lam_inits),
        out_shape=jax.ShapeDtypeStruct((depth, 1, TM), F32),
        name="attn_lambda",
    )(attn_lambda)


def _attn_kernel(lam_ref, qt0_ref, qt1_ref, k_ref, vt_ref, o_ref, *, n_ctx, lam_init):
    i = pl.program_id(2)
    streams = [(hd, qt_ref[0, hd * V_DIM:(hd + 1) * V_DIM, :])
               for hd in range(HEADS_PER_STEP) for qt_ref in (qt0_ref, qt1_ref)]

    def attend(n_keys):
        m = [None] * len(streams)
        u = [None] * len(streams)

        def score(c0, j):
            hd, qt = streams[j]
            return jnp.dot(k_ref[0, c0:c0 + KEY_CHUNK, hd * V_DIM:(hd + 1) * V_DIM], qt,
                           preferred_element_type=F32)

        starts = list(range(0, n_keys, KEY_CHUNK))
        pending = [[score(c0, j) for c0 in starts[:SCORE_LOOKAHEAD]] for j in range(len(streams))]
        def value_update(j, hd, c0, pt, rescale):
            part = jnp.dot(vt_ref[0, hd * VT_ROWS:(hd + 1) * VT_ROWS, c0:c0 + KEY_CHUNK], pt,
                           preferred_element_type=F32)
            u[j] = part if rescale is None else u[j] * rescale + part

        deferred = None
        for n, c0 in enumerate(starts):
            for j, (hd, _) in enumerate(streams):
                if n + SCORE_LOOKAHEAD < len(starts):
                    pending[j].append(score(starts[n + SCORE_LOOKAHEAD], j))
                st = pending[j].pop(0)
                m_chunk = jnp.max(st, axis=0, keepdims=True)
                m_new = m_chunk if m[j] is None else jnp.maximum(m[j], m_chunk)
                pt = jnp.exp2(st - m_new).astype(BF16)
                rescale = None if m[j] is None else jnp.exp2(m[j] - m_new)
                m[j] = m_new
                if deferred is not None:
                    value_update(*deferred)
                deferred = (j, hd, c0, pt, rescale)
        value_update(*deferred)
        for hd in range(HEADS_PER_STEP):
            u0, u1 = u[2 * hd], u[2 * hd + 1]
            ot = u0[:V_DIM] * (1.0 / u0[V_DIM:V_DIM + 1]) - u1[:V_DIM] * (lam_ref[...] / u1[V_DIM:V_DIM + 1])
            ot = ot * lax.rsqrt(jnp.mean(ot * ot, axis=0, keepdims=True) + EPS) * (1.0 - lam_init)
            o_ref[0, :, hd * V_DIM:(hd + 1) * V_DIM] = ot.T.astype(BF16)

    @pl.when(i == 0)
    def _():
        attend(n_ctx)

    @pl.when(i > 0)
    def _():
        attend(k_ref.shape[1])


def _attention(lam_row, qt0, qt1, k, vt, n_ctx, lam_init):
    b, t, d = k.shape
    assert n_ctx == TM
    hps = HEADS_PER_STEP
    qt = pl.BlockSpec((1, hps * V_DIM, TM), lambda bb, hh, i: (bb, hh, i))
    keys = pl.BlockSpec((1, t, hps * V_DIM), lambda bb, hh, i: (bb, 0, hh))
    vals = pl.BlockSpec((1, hps * VT_ROWS, t), lambda bb, hh, i: (bb, hh, 0))
    return pl.pallas_call(
        functools.partial(_attn_kernel, n_ctx=n_ctx, lam_init=lam_init),
        grid=(b, d // (hps * V_DIM), t // TM),
        in_specs=[pl.BlockSpec(lam_row.shape, lambda bb, hh, i: (0, 0)), qt, qt, keys, vals],
        out_specs=pl.BlockSpec((1, TM, hps * V_DIM), lambda bb, hh, i: (bb, i, hh)),
        out_shape=jax.ShapeDtypeStruct((b, t, d), BF16),
        compiler_params=_params(3),
        name="diffattn",
    )(lam_row, qt0, qt1, k, vt)


def _merge_kernel(x_ref, mod_ref, h_ref, ya_ref, yb_ref, yc_ref, wb_ref, wg_ref, bg_ref, wo_ref, o_ref):
    d = x_ref.shape[-1]
    g = jax.nn.sigmoid(jnp.dot(h_ref[0], wg_ref[0].astype(BF16), preferred_element_type=F32) + bg_ref[...])
    m = g[:, :d] * jnp.dot(ya_ref[0], wb_ref[0, :d, :].astype(BF16), preferred_element_type=F32)
    m = m + g[:, d:2 * d] * jnp.dot(yb_ref[0], wb_ref[0, d:2 * d, :].astype(BF16), preferred_element_type=F32)
    m = m + g[:, 2 * d:] * jnp.dot(yc_ref[0], wb_ref[0, 2 * d:, :].astype(BF16), preferred_element_type=F32)
    out = jnp.dot(m.astype(BF16), wo_ref[0].astype(BF16), preferred_element_type=F32)
    o_ref[0] = x_ref[0] + mod_ref[0, 2:3, :] * out


def _merge(x_all, mod, h, ya, yb, yc, w_branch, w_gate, b_gate, w_out, layer, tile0):
    b, t, d = x_all.shape
    nt = t // TM - tile0
    tok = pl.BlockSpec((1, TM, d), lambda bb, i: (bb, i + tile0, 0))
    return pl.pallas_call(
        _merge_kernel,
        grid=(b, nt),
        in_specs=[
            tok,
            pl.BlockSpec((1, N_MOD, d), lambda bb, i: (jnp.where(i + tile0 == 0, b, bb), 0, 0)),
            tok, tok, tok, tok,
            _layer_spec(w_branch.shape, layer), _layer_spec(w_gate.shape, layer), _const_spec(b_gate.shape),
            _layer_spec(w_out.shape, layer),
        ],
        out_specs=pl.BlockSpec((1, TM, d), lambda bb, i: (bb, i, 0)),
        out_shape=jax.ShapeDtypeStruct((b, nt * TM, d), F32),
        compiler_params=_params(2),
        name="merge",
    )(x_all, mod, h, ya, yb, yc, w_branch, w_gate, b_gate, w_out)


def _ffn_kernel(x_ref, mod_ref, w1_ref, w3_ref, w2_ref, fg_ref, o_ref, *, final_norm):
    x = x_ref[0]
    hb = (_rms(x) * (1.0 + mod_ref[0, 4:5, :]) + mod_ref[0, 3:4, :]).astype(BF16)
    a = jnp.dot(hb, w1_ref[0].astype(BF16), preferred_element_type=F32)
    g = jnp.dot(hb, w3_ref[0].astype(BF16), preferred_element_type=F32)
    s = (a * jax.nn.sigmoid(a) * g).astype(BF16)
    y = x + mod_ref[0, 5:6, :] * jnp.dot(s, w2_ref[0].astype(BF16), preferred_element_type=F32)
    if final_norm:
        y = _rms(y) * fg_ref[...]
        n_piece, steps, d = o_ref.shape[1:]
        o_ref[0] = jnp.swapaxes(y.reshape(steps, n_piece, d), 0, 1)
    else:
        o_ref[0] = y


def _ffn(x_all, mod, w1, w3, w2, layer, final_g, ctx_tiles, final_norm):
    b, t, d = x_all.shape
    tok = pl.BlockSpec((1, TM, d), lambda bb, i: (bb, i, 0))
    if final_norm:
        steps = TM // SUBLANES
        out_spec = pl.BlockSpec((1, SUBLANES, steps, d), lambda bb, i: (bb, 0, i, 0))
        out_shape = jax.ShapeDtypeStruct((b, SUBLANES, t // SUBLANES, d), F32)
    else:
        out_spec, out_shape = tok, jax.ShapeDtypeStruct((b, t, d), F32)
    out = pl.pallas_call(
        functools.partial(_ffn_kernel, final_norm=final_norm),
        grid=(b, t // TM),
        in_specs=[
            tok,
            pl.BlockSpec((1, N_MOD, d), lambda bb, i: (jnp.where(i < ctx_tiles, b, bb), 0, 0)),
            _layer_spec(w1.shape, layer), _layer_spec(w3.shape, layer), _layer_spec(w2.shape, layer),
            _const_spec(final_g.shape),
        ],
        out_specs=out_spec,
        out_shape=out_shape,
        compiler_params=_params(2),
        name="ffn",
    )(x_all, mod, w1, w3, w2, final_g)
    return out.reshape(b, t, d)


def _piece_order(n):
    r = np.arange(n, dtype=np.int64)
    return (r % SUBLANES) * (n // SUBLANES) + r // SUBLANES


def _rope_tables(n_ctx, n_lat):
    p = _piece_order(n_lat)
    rows = (p // GRID_W).astype(np.float32)
    cols = (p % GRID_W).astype(np.float32)
    n_freq = HEAD_DIM // 4
    inv = np.float32(ROPE_BASE) ** (-np.arange(n_freq, dtype=np.float32) / np.float32(n_freq))
    ang = np.concatenate([rows[:, None] * inv, cols[:, None] * inv], axis=-1)
    cos, sin = np.cos(ang), np.sin(ang)
    reps = LANES // HEAD_DIM
    cos_l = np.tile(np.concatenate([cos, cos], axis=-1), (1, reps))
    sin_l = np.tile(np.concatenate([-sin, sin], axis=-1), (1, reps))
    cos_t = np.concatenate([np.ones((n_ctx, LANES), np.float32), cos_l], axis=0)
    sin_t = np.concatenate([np.zeros((n_ctx, LANES), np.float32), sin_l], axis=0)
    return jnp.asarray(cos_t, F32), jnp.asarray(sin_t, F32)


def _dft_cos_sin(n, k, t):
    ang = ((k[:, None] * t[None, :]) % n) * (2.0 * math.pi / n)
    return np.cos(ang), np.sin(ang)


def _dft_parity_tables(n):
    steps, half = n // SUBLANES, SUBLANES // 2
    r_in = np.arange(n // 2, dtype=np.int64)
    step_in, piece_in = r_in // half, r_in % half
    t = piece_in * steps + step_in
    planes = []
    for par in range(2):
        step_out = np.arange(par, steps, 2, dtype=np.int64)
        k = (np.arange(SUBLANES, dtype=np.int64)[None, :] * steps + step_out[:, None]).reshape(-1)
        c, s = _dft_cos_sin(n, k, t)
        sign = np.where((step_in % 2 == 1) & (par == 1), -1.0, 1.0)
        planes.append(np.concatenate([c * sign, -s * sign], axis=1))
    return np.stack(planes)


def _dft_position_matrix(n):
    return jnp.asarray(_dft_parity_tables(n), F32).astype(BF16)


def kernel(x, c, ctx, c_ctx, ada_w, ada_b, w_in, rnn_conv_w, rnn_conv_b, rnn_wr, rnn_br, rnn_wi, rnn_bi,
           rnn_lambda, attn_lambda, w_branch, w_gate, b_gate, w_out, ffn_w1, ffn_w3, ffn_w2, final_g):
    b, n_lat, d = x.shape
    n_ctx = ctx.shape[1]
    depth = ada_w.shape[0]
    ctx_tiles = n_ctx // TM
    assert n_ctx == TM and n_lat % TM == 0

    cond = jnp.zeros((2 * SUBLANES, d), F32).at[:b].set(c).at[b].set(c_ctx)
    mod_all = _adaln(cond, ada_w, ada_b.reshape(depth, 1, N_MOD * d))

    cos_t, sin_t = _rope_tables(n_ctx, n_lat)
    ch = np.arange(FOURIER_GD, dtype=np.int64)
    cc, sc = _dft_cos_sin(FOURIER_GD, ch, ch)
    wf = jnp.asarray(np.concatenate([cc, sc], axis=1), F32).astype(BF16)
    a_lat = _dft_position_matrix(n_lat)
    a_ctx = _dft_position_matrix(n_ctx)
    fg = final_g.reshape(1, d)

    x_all = (ctx, x)
    lam_inits = tuple(0.8 - 0.6 * math.exp(-0.3 * l) for l in range(depth))
    lam_rows = _attn_lambda(attn_lambda, lam_inits)
    for l in range(depth):
        last = l == depth - 1
        mod = mod_all[l].reshape(2 * SUBLANES, N_MOD, d)
        outs = _inproj(x_all, mod, w_in, l, wf, cos_t, sin_t)
        if l == 0:
            x_all, outs = outs[0], outs[1:]
        h, xr, gr, xce, xse, xco, xso, qt0, qt1, k, vt = outs
        ya = _rglru(xr, gr, rnn_conv_w[l], rnn_conv_b[l].reshape(1, d), rnn_wr[l].astype(BF16), rnn_br[l],
                    rnn_wi[l].astype(BF16), rnn_bi[l], rnn_lambda[l], n_ctx)
        yb = _fourier(a_lat, a_ctx, xce, xse, xco, xso, n_ctx)
        yc = _attention(lam_rows[l], qt0, qt1, k, vt, n_ctx, lam_inits[l])
        tile0 = ctx_tiles if last else 0
        x_mid = _merge(x_all, mod, h, ya, yb, yc, w_branch, w_gate, b_gate[l].reshape(1, -1), w_out, l, tile0)
        x_all = _ffn(x_mid, mod, ffn_w1, ffn_w3, ffn_w2, l, fg, ctx_tiles - tile0, last)
    return x_all
```

```python
import functools
import math

import jax
import jax.numpy as jnp
import numpy as np
from jax import lax
from jax.experimental import pallas as pl
from jax.experimental.pallas import tpu as pltpu

F32 = jnp.float32
BF16 = jnp.bfloat16

EPS = 1e-6
N_MOD = 6
GRID_W = 64
RNN_BLOCK = 128
CONV_W = 4
LRU_C = 8.0
FOURIER_GD = 128
N_HEADS = 8
HEAD_DIM = 64
V_DIM = 2 * HEAD_DIM
ROPE_BASE = 10000.0

LANES = 128
SUBLANES = 8
TM = 256
SCAN_UNROLL = 8
KEY_CHUNK = 256
SCORE_LOOKAHEAD = 1
HEADS_PER_STEP = 4
BF16_SUBLANES = 16
VT_ROWS = V_DIM + BF16_SUBLANES
LOG2_E = math.log2(math.e)
VMEM_LIMIT = 56 * 1024 * 1024


def _params(n_axes):
    return pltpu.CompilerParams(
        dimension_semantics=("parallel",) * n_axes, vmem_limit_bytes=VMEM_LIMIT)


def _const_spec(shape):
    nd = len(shape)
    return pl.BlockSpec(shape, lambda *_: (0,) * nd, pipeline_mode=pl.Buffered(1))


def _layer_spec(shape, layer):
    nd = len(shape)
    return pl.BlockSpec((1,) + tuple(shape[1:]), lambda *_: (layer,) + (0,) * (nd - 1),
                        pipeline_mode=pl.Buffered(1))


def _rms(x):
    return x * lax.rsqrt(jnp.mean(x * x, axis=-1, keepdims=True) + EPS)


def _adaln_kernel(c_ref, w_ref, b_ref, o_ref):
    c = c_ref[...]
    s = (c * jax.nn.sigmoid(c)).astype(BF16)
    o_ref[0] = jnp.dot(s, w_ref[0].astype(BF16), preferred_element_type=F32) + b_ref[0]


def _adaln(cond, ada_w, ada_b):
    depth, d, n = ada_w.shape
    r = cond.shape[0]
    tn = 1536
    return pl.pallas_call(
        _adaln_kernel,
        grid=(depth, n // tn),
        in_specs=[
            pl.BlockSpec((r, d), lambda l, j: (0, 0)),
            pl.BlockSpec((1, d, tn), lambda l, j: (l, 0, j)),
            pl.BlockSpec((1, 1, tn), lambda l, j: (l, 0, j)),
        ],
        out_specs=pl.BlockSpec((1, r, tn), lambda l, j: (l, 0, j)),
        out_shape=jax.ShapeDtypeStruct((depth, r, n), F32),
        compiler_params=_params(2),
        name="adaln",
    )(cond, ada_w, ada_b)


def _inproj_kernel(*refs, from_time_order):
    if from_time_order:
        (ctx_ref, lat_ref, mod_ref, w_ref, wf_ref, cos_ref, sin_ref,
         x_ref, h_ref, xr_ref, gr_ref, xce_ref, xse_ref, xco_ref, xso_ref, qt0_ref, qt1_ref, k_ref, vt_ref) = refs
        blk = jnp.where(pl.program_id(1) == 0, ctx_ref[0], lat_ref[0])
        x = jnp.swapaxes(blk, 0, 1).reshape(x_ref.shape[1:])
        x_ref[0] = x
    else:
        (x_ref, mod_ref, w_ref, wf_ref, cos_ref, sin_ref,
         h_ref, xr_ref, gr_ref, xce_ref, xse_ref, xco_ref, xso_ref, qt0_ref, qt1_ref, k_ref, vt_ref) = refs
        x = x_ref[0]
    d = x.shape[-1]
    h = _rms(x) * (1.0 + mod_ref[0, 1:2, :]) + mod_ref[0, 0:1, :]
    hb = h.astype(BF16)
    h_ref[0] = hb

    def proj(j):
        return jnp.dot(hb, w_ref[0, :, j * d:(j + 1) * d].astype(BF16), preferred_element_type=F32)

    xr_ref[0] = proj(0)
    gr_ref[0] = proj(1)

    xf = proj(2).astype(BF16)
    steps = x.shape[0] // SUBLANES
    low = lax.broadcasted_iota(jnp.int32, (SUBLANES, 2 * FOURIER_GD), 0) < SUBLANES // 2
    for g in range(d // FOURIER_GD):
        sl = slice(g * FOURIER_GD, (g + 1) * FOURIER_GD)
        z = jnp.dot(xf[:, sl], wf_ref[...], preferred_element_type=F32)
        z3 = z.reshape(steps, SUBLANES, 2 * FOURIER_GD)
        zr = pltpu.roll(z3, SUBLANES // 2, 1)
        for folded, c_ref, s_ref in ((z3 + zr, xce_ref, xse_ref), (z3 - zr, xco_ref, xso_ref)):
            pair = folded.reshape(steps // 2, 2, SUBLANES, 2 * FOURIER_GD)
            kept = jnp.where(low, pair[:, 0], pair[:, 1]).reshape(x.shape[0] // 2, 2 * FOURIER_GD)
            c_ref[0, :, sl] = kept[:, :FOURIER_GD].astype(BF16)
            s_ref[0, :, sl] = kept[:, FOURIER_GD:].astype(BF16)

    cos = cos_ref[...]
    sin = sin_ref[...]
    lane = lax.broadcasted_iota(jnp.int32, cos.shape, 1)
    first_half = (lane % HEAD_DIM) < (HEAD_DIM // 2)

    def rope(th):
        swapped = jnp.where(first_half,
                            pltpu.roll(th, LANES - HEAD_DIM // 2, 1),
                            pltpu.roll(th, HEAD_DIM // 2, 1))
        return th * cos + swapped * sin

    q = proj(3)
    row = lax.broadcasted_iota(jnp.int32, (V_DIM, q.shape[0]), 0)
    zero_t = jnp.zeros((V_DIM, q.shape[0]), F32)
    for hh in range(d // V_DIM):
        sl = slice(hh * V_DIM, (hh + 1) * V_DIM)
        qt = (rope(q[:, sl]) * (HEAD_DIM ** -0.5 * LOG2_E)).T
        qt0_ref[0, sl, :] = jnp.where(row < HEAD_DIM, qt, zero_t).astype(BF16)
        qt1_ref[0, sl, :] = jnp.where(row >= HEAD_DIM, qt, zero_t).astype(BF16)

    k = proj(4)
    for hh in range(d // LANES):
        sl = slice(hh * LANES, (hh + 1) * LANES)
        k_ref[0, :, sl] = rope(k[:, sl]).astype(BF16)

    v = proj(5)
    ones = jnp.ones((VT_ROWS - V_DIM, v.shape[0]), BF16)
    for hh in range(d // V_DIM):
        vt_ref[0, hh * VT_ROWS:hh * VT_ROWS + V_DIM, :] = v[:, hh * V_DIM:(hh + 1) * V_DIM].T.astype(BF16)
        vt_ref[0, hh * VT_ROWS + V_DIM:(hh + 1) * VT_ROWS, :] = ones


def _inproj(x_src, mod, w_in, layer, wf, cos_t, sin_t):
    from_time_order = isinstance(x_src, tuple)
    d = mod.shape[-1]
    tok = pl.BlockSpec((1, TM, d), lambda bb, i: (bb, i, 0))
    if from_time_order:
        ctx, x = x_src
        b, t = x.shape[0], ctx.shape[1] + x.shape[1]
        steps = TM // SUBLANES
        as_pieces = lambda v: v.reshape(b, SUBLANES, v.shape[1] // SUBLANES, d)
        srcs = [as_pieces(ctx), as_pieces(x)]
        src_specs = [pl.BlockSpec((1, SUBLANES, steps, d), lambda bb, i: (bb, 0, 0, 0)),
                     pl.BlockSpec((1, SUBLANES, steps, d), lambda bb, i: (bb, 0, jnp.maximum(i - 1, 0), 0))]
    else:
        b, t, _ = x_src.shape
        srcs, src_specs = [x_src], [tok]
    nt = t // TM
    vt_rows = d // V_DIM * VT_ROWS
    tok_t = lambda rows: pl.BlockSpec((1, rows, TM), lambda bb, i: (bb, 0, i))
    f32_out = jax.ShapeDtypeStruct((b, t, d), F32)
    bf_out = jax.ShapeDtypeStruct((b, t, d), BF16)
    half_tok = pl.BlockSpec((1, TM // 2, d), lambda bb, i: (bb, i, 0))
    half_out = jax.ShapeDtypeStruct((b, t // 2, d), BF16)
    extra = from_time_order * 1
    return pl.pallas_call(
        functools.partial(_inproj_kernel, from_time_order=from_time_order),
        grid=(b, nt),
        in_specs=src_specs + [
            pl.BlockSpec((1, N_MOD, d), lambda bb, i: (jnp.where(i == 0, b, bb), 0, 0)),
            _layer_spec(w_in.shape, layer),
            _const_spec(wf.shape),
            pl.BlockSpec((TM, LANES), lambda bb, i: (i, 0)),
            pl.BlockSpec((TM, LANES), lambda bb, i: (i, 0)),
        ],
        out_specs=[tok] * (3 + extra) + [half_tok] * 4 + [tok_t(d), tok_t(d), tok, tok_t(vt_rows)],
        out_shape=[f32_out] * extra + [bf_out, f32_out, f32_out] + [half_out] * 4 + [
                                       jax.ShapeDtypeStruct((b, d, t), BF16),
                                       jax.ShapeDtypeStruct((b, d, t), BF16), bf_out,
                                       jax.ShapeDtypeStruct((b, vt_rows, t), BF16)],
        compiler_params=_params(2),
        name="inproj",
    )(*srcs, mod, w_in, wf, cos_t, sin_t)


def _log_sigmoid(x):
    return jnp.minimum(x, 0.0) - jnp.log(1.0 + jnp.exp(-jnp.abs(x)))


def _gelu_tanh(x):
    c0 = math.sqrt(2.0 / math.pi)
    half_x = 0.5 * x
    return half_x * jnp.tanh(x * (c0 + (c0 * 0.044715) * (x * x))) + half_x


def _rglru_kernel(xr_ref, gr_ref, cw_ref, cb_ref, wr_ref, br_ref, wi_ref, bi_ref, lam_ref, ya_ref,
                  xpad, maps0, maps1, h0, h1, *, n_ctx):
    t_all, c = xr_ref.shape[1], xr_ref.shape[2]
    n_piece = SUBLANES
    w = [cw_ref[k:k + 1, :] for k in range(CONV_W)]
    sub = lax.broadcasted_iota(jnp.int32, (n_piece, c), 0)
    zeros = jnp.zeros((n_piece, c), F32)

    def from_prev_piece(v):
        return jnp.where(sub >= 1, pltpu.roll(v, 1, 0), 0.0)

    def from_next_piece(v):
        return jnp.where(sub < n_piece - 1, pltpu.roll(v, n_piece - 1, 0), 0.0)

    def run_sequence(base, n, state_f, state_r):
        steps = n // n_piece
        x3 = xr_ref[0, base:base + n, :].reshape(steps, n_piece, c)
        xpad[2:steps + 2] = x3
        xpad[0] = from_prev_piece(x3[steps - 2])
        xpad[1] = from_prev_piece(x3[steps - 1])
        xpad[steps + 2] = from_next_piece(x3[0])
        u = xpad[0:steps] * w[0]
        u = u + xpad[1:steps + 1] * w[1]
        u = u + xpad[2:steps + 2] * w[2]
        u = u + xpad[3:steps + 3] * w[3]
        u = (u + cb_ref[...]).reshape(n, c)
        ub = u.astype(BF16)
        half_u = 0.5 * u
        totals = []
        for dd, maps in enumerate((maps0, maps1)):
            tr = jnp.tanh(0.5 * (jnp.dot(ub, wr_ref[dd, 0], preferred_element_type=F32) + br_ref[dd:dd + 1, :]))
            ti = jnp.tanh(0.5 * (jnp.dot(ub, wi_ref[dd, 0], preferred_element_type=F32) + bi_ref[dd:dd + 1, :]))
            e2 = (0.5 * LRU_C * LOG2_E) * _log_sigmoid(lam_ref[dd:dd + 1, :])
            a = jnp.exp2(tr * e2 + e2)
            y = 1.0 - a * a
            root = jnp.where(y > 0.0, y * lax.rsqrt(y), 0.0)
            p = a.reshape(steps, n_piece, c)
            q = ((ti * half_u + half_u) * root).reshape(steps, n_piece, c)
            m = steps
            while m > 1:
                p = p.reshape(m // 2, 2, n_piece, c)
                q = q.reshape(m // 2, 2, n_piece, c)
                first, second = (0, 1) if dd == 0 else (1, 0)
                if m == steps:
                    maps[0:m // 2, 0] = p[:, first]
                    maps[0:m // 2, 1] = q[:, first]
                q = p[:, second] * q[:, first] + q[:, second]
                p = p[:, second] * p[:, first]
                m //= 2
                if m == steps // 2:
                    maps[0:m, 2] = p
                    maps[0:m, 3] = q
            totals.append((p[0], q[0]))
        (pf, hf), (pr, hr) = totals

        init_f = zeros
        for s in range(n_piece):
            init_f = jnp.where(sub == s, state_f, init_f)
            state_f = hf[s:s + 1, :] + pf[s:s + 1, :] * state_f
        init_r = zeros
        for s in reversed(range(n_piece)):
            init_r = jnp.where(sub == s, state_r, init_r)
            state_r = hr[s:s + 1, :] + pr[s:s + 1, :] * state_r

        def piece_states(k, carry):
            hf, hr = carry
            mf = maps0[k]
            mid = mf[0] * hf + mf[1]
            hf = mf[2] * hf + mf[3]
            h0[k] = jnp.stack([mid, hf])
            kr = steps // 2 - 1 - k
            mr = maps1[kr]
            mid = mr[0] * hr + mr[1]
            hr = mr[2] * hr + mr[3]
            h1[kr] = jnp.stack([hr, mid])
            return hf, hr

        lax.fori_loop(0, steps // 2, piece_states, (init_f, init_r), unroll=SCAN_UNROLL)
        h = (h0[0:steps // 2] + h1[0:steps // 2]).reshape(n, c)
        ya_ref[0, base:base + n, :] = (h * _gelu_tanh(gr_ref[0, base:base + n, :])).astype(BF16)
        return state_f, state_r

    zero_state = jnp.zeros((1, c), F32)
    state_f, state_r = run_sequence(0, n_ctx, zero_state, zero_state)
    run_sequence(n_ctx, t_all - n_ctx, state_f, state_r)


def _rglru(xr, gr, conv_w, conv_b, wr, br, wi, bi, lam, n_ctx):
    b, t, d = xr.shape
    c = RNN_BLOCK
    steps = max(n_ctx, t - n_ctx) // SUBLANES
    seq = pl.BlockSpec((1, t, c), lambda bb, g: (bb, 0, g))
    vec = lambda rows: pl.BlockSpec((rows, c), lambda bb, g: (0, g))
    mat = pl.BlockSpec((2, 1, c, c), lambda bb, g: (0, g, 0, 0))
    return pl.pallas_call(
        functools.partial(_rglru_kernel, n_ctx=n_ctx),
        grid=(b, d // c),
        in_specs=[seq, seq, vec(CONV_W), vec(1), mat, vec(2), mat, vec(2), vec(2)],
        out_specs=seq,
        out_shape=jax.ShapeDtypeStruct((b, t, d), BF16),
        scratch_shapes=([pltpu.VMEM((steps + CONV_W - 1, SUBLANES, c), F32)]
                        + [pltpu.VMEM((steps // 2, 4, SUBLANES, c), F32)] * 2
                        + [pltpu.VMEM((steps // 2, 2, SUBLANES, c), F32)] * 2),
        compiler_params=_params(2),
        name="rglru",
    )(xr, gr, conv_w, conv_b, wr, br, wi, bi, lam)


def _fourier_kernel(al_ref, ac_ref, xce_ref, xse_ref, xco_ref, xso_ref, y_ref, *, n_ctx, scale_ctx, scale_lat):
    i = pl.program_id(1)
    half_ctx = n_ctx // 2
    half_lat = xce_ref.shape[1] - half_ctx
    rows, d = y_ref.shape[1], y_ref.shape[2]

    def mix(a_ref, lo, n2, scale):
        by_parity = []
        for par, (c_ref, s_ref) in enumerate(((xce_ref, xse_ref), (xco_ref, xso_ref))):
            y = jnp.dot(a_ref[par, :, :n2], c_ref[0, lo:lo + n2, :], preferred_element_type=F32)
            y = y + jnp.dot(a_ref[par, :, n2:], s_ref[0, lo:lo + n2, :], preferred_element_type=F32)
            by_parity.append((y * scale).reshape(rows // (2 * SUBLANES), SUBLANES, d))
        y_ref[0] = jnp.stack(by_parity, axis=1).reshape(rows, d).astype(BF16)

    @pl.when(i == 0)
    def _():
        mix(ac_ref, 0, half_ctx, scale_ctx)

    @pl.when(i > 0)
    def _():
        mix(al_ref, half_ctx, half_lat, scale_lat)


def _fourier(a_lat, a_ctx, xce, xse, xco, xso, n_ctx):
    b, t2, d = xce.shape
    t = 2 * t2
    n_lat = t - n_ctx
    assert n_ctx == TM
    slab = pl.BlockSpec((1, t2, d), lambda bb, i: (bb, 0, 0))
    return pl.pallas_call(
        functools.partial(_fourier_kernel, n_ctx=n_ctx,
                          scale_ctx=(n_ctx * FOURIER_GD) ** -0.5,
                          scale_lat=(n_lat * FOURIER_GD) ** -0.5),
        grid=(b, t // TM),
        in_specs=[
            pl.BlockSpec((2, TM // 2, n_lat), lambda bb, i: (0, jnp.maximum(i - 1, 0), 0)),
            _const_spec(a_ctx.shape),
            slab, slab, slab, slab,
        ],
        out_specs=pl.BlockSpec((1, TM, d), lambda bb, i: (bb, i, 0)),
        out_shape=jax.ShapeDtypeStruct((b, t, d), BF16),
        compiler_params=_params(2),
        name="fourier",
    )(a_lat, a_ctx, xce, xse, xco, xso)


def _attn_lambda_kernel(lv_ref, o_ref, *, lam_inits):
    for l, lam_init in enumerate(lam_inits):
        lv = lv_ref[l]
        lam = (jnp.exp(jnp.sum(lv[0:1] * lv[1:2], axis=1, keepdims=True))
               - jnp.exp(jnp.sum(lv[2:3] * lv[3:4], axis=1, keepdims=True)) + lam_init)
        o_ref[l] = jnp.broadcast_to(lam, o_ref.shape[1:])


def _attn_lambda(attn_lambda, lam_inits):
    depth = attn_lambda.shape[0]
    return pl.pallas_call(
        functools.partial(_attn_lambda_kernel, lam_inits=lam_inits),
        out_shape=jax.ShapeDtypeStruct((depth, 1, TM), F32),
        name="attn_lambda",
    )(attn_lambda)


def _attn_kernel(lam_ref, qt0_ref, qt1_ref, k_ref, vt_ref, o_ref, *, n_ctx, lam_init):
    i = pl.program_id(2)
    streams = [(hd, qt_ref[0, hd * V_DIM:(hd + 1) * V_DIM, :])
               for hd in range(HEADS_PER_STEP) for qt_ref in (qt0_ref, qt1_ref)]

    def attend(n_keys):
        m = [None] * len(streams)
        u = [None] * len(streams)

        def score(c0, j):
            hd, qt = streams[j]
            return jnp.dot(k_ref[0, c0:c0 + KEY_CHUNK, hd * V_DIM:(hd + 1) * V_DIM], qt,
                           preferred_element_type=F32)

        starts = list(range(0, n_keys, KEY_CHUNK))
        pending = [[score(c0, j) for c0 in starts[:SCORE_LOOKAHEAD]] for j in range(len(streams))]
        def value_update(j, hd, c0, pt, rescale):
            part = jnp.dot(vt_ref[0, hd * VT_ROWS:(hd + 1) * VT_ROWS, c0:c0 + KEY_CHUNK], pt,
                           preferred_element_type=F32)
            u[j] = part if rescale is None else u[j] * rescale + part

        deferred = None
        for n, c0 in enumerate(starts):
            for j, (hd, _) in enumerate(streams):
                if n + SCORE_LOOKAHEAD < len(starts):
                    pending[j].append(score(starts[n + SCORE_LOOKAHEAD], j))
                st = pending[j].pop(0)
                m_chunk = jnp.max(st, axis=0, keepdims=True)
                m_new = m_chunk if m[j] is None else jnp.maximum(m[j], m_chunk)
                pt = jnp.exp2(st - m_new).astype(BF16)
                rescale = None if m[j] is None else jnp.exp2(m[j] - m_new)
                m[j] = m_new
                if deferred is not None:
                    value_update(*deferred)
                deferred = (j, hd, c0, pt, rescale)
        value_update(*deferred)
        for hd in range(HEADS_PER_STEP):
            u0, u1 = u[2 * hd], u[2 * hd + 1]
            ot = u0[:V_DIM] * (1.0 / u0[V_DIM:V_DIM + 1]) - u1[:V_DIM] * (lam_ref[...] / u1[V_DIM:V_DIM + 1])
            ot = ot * lax.rsqrt(jnp.mean(ot * ot, axis=0, keepdims=True) + EPS) * (1.0 - lam_init)
            o_ref[0, :, hd * V_DIM:(hd + 1) * V_DIM] = ot.T.astype(BF16)

    @pl.when(i == 0)
    def _():
        attend(n_ctx)

    @pl.when(i > 0)
    def _():
        attend(k_ref.shape[1])


def _attention(lam_row, qt0, qt1, k, vt, n_ctx, lam_init):
    b, t, d = k.shape
    assert n_ctx == TM
    hps = HEADS_PER_STEP
    qt = pl.BlockSpec((1, hps * V_DIM, TM), lambda bb, hh, i: (bb, hh, i))
    keys = pl.BlockSpec((1, t, hps * V_DIM), lambda bb, hh, i: (bb, 0, hh))
    vals = pl.BlockSpec((1, hps * VT_ROWS, t), lambda bb, hh, i: (bb, hh, 0))
    return pl.pallas_call(
        functools.partial(_attn_kernel, n_ctx=n_ctx, lam_init=lam_init),
        grid=(b, d // (hps * V_DIM), t // TM),
        in_specs=[pl.BlockSpec(lam_row.shape, lambda bb, hh, i: (0, 0)), qt, qt, keys, vals],
        out_specs=pl.BlockSpec((1, TM, hps * V_DIM), lambda bb, hh, i: (bb, i, hh)),
        out_shape=jax.ShapeDtypeStruct((b, t, d), BF16),
        compiler_params=_params(3),
        name="diffattn",
    )(lam_row, qt0, qt1, k, vt)


def _merge_kernel(x_ref, mod_ref, h_ref, ya_ref, yb_ref, yc_ref, wb_ref, wg_ref, bg_ref, wo_ref, o_ref):
    d = x_ref.shape[-1]
    g = jax.nn.sigmoid(jnp.dot(h_ref[0], wg_ref[0].astype(BF16), preferred_element_type=F32) + bg_ref[...])
    m = g[:, :d] * jnp.dot(ya_ref[0], wb_ref[0, :d, :].astype(BF16), preferred_element_type=F32)
    m = m + g[:, d:2 * d] * jnp.dot(yb_ref[0], wb_ref[0, d:2 * d, :].astype(BF16), preferred_element_type=F32)
    m = m + g[:, 2 * d:] * jnp.dot(yc_ref[0], wb_ref[0, 2 * d:, :].astype(BF16), preferred_element_type=F32)
    out = jnp.dot(m.astype(BF16), wo_ref[0].astype(BF16), preferred_element_type=F32)
    o_ref[0] = x_ref[0] + mod_ref[0, 2:3, :] * out


def _merge(x_all, mod, h, ya, yb, yc, w_branch, w_gate, b_gate, w_out, layer, tile0):
    b, t, d = x_all.shape
    nt = t // TM - tile0
    tok = pl.BlockSpec((1, TM, d), lambda bb, i: (bb, i + tile0, 0))
    return pl.pallas_call(
        _merge_kernel,
        grid=(b, nt),
        in_specs=[
            tok,
            pl.BlockSpec((1, N_MOD, d), lambda bb, i: (jnp.where(i + tile0 == 0, b, bb), 0, 0)),
            tok, tok, tok, tok,
            _layer_spec(w_branch.shape, layer), _layer_spec(w_gate.shape, layer), _const_spec(b_gate.shape),
            _layer_spec(w_out.shape, layer),
        ],
        out_specs=pl.BlockSpec((1, TM, d), lambda bb, i: (bb, i, 0)),
        out_shape=jax.ShapeDtypeStruct((b, nt * TM, d), F32),
        compiler_params=_params(2),
        name="merge",
    )(x_all, mod, h, ya, yb, yc, w_branch, w_gate, b_gate, w_out)


def _ffn_kernel(x_ref, mod_ref, w1_ref, w3_ref, w2_ref, fg_ref, o_ref, *, final_norm):
    x = x_ref[0]
    hb = (_rms(x) * (1.0 + mod_ref[0, 4:5, :]) + mod_ref[0, 3:4, :]).astype(BF16)
    a = jnp.dot(hb, w1_ref[0].astype(BF16), preferred_element_type=F32)
    g = jnp.dot(hb, w3_ref[0].astype(BF16), preferred_element_type=F32)
    s = (a * jax.nn.sigmoid(a) * g).astype(BF16)
    y = x + mod_ref[0, 5:6, :] * jnp.dot(s, w2_ref[0].astype(BF16), preferred_element_type=F32)
    if final_norm:
        y = _rms(y) * fg_ref[...]
        n_piece, steps, d = o_ref.shape[1:]
        o_ref[0] = jnp.swapaxes(y.reshape(steps, n_piece, d), 0, 1)
    else:
        o_ref[0] = y


def _ffn(x_all, mod, w1, w3, w2, layer, final_g, ctx_tiles, final_norm):
    b, t, d = x_all.shape
    tok = pl.BlockSpec((1, TM, d), lambda bb, i: (bb, i, 0))
    if final_norm:
        steps = TM // SUBLANES
        out_spec = pl.BlockSpec((1, SUBLANES, steps, d), lambda bb, i: (bb, 0, i, 0))
        out_shape = jax.ShapeDtypeStruct((b, SUBLANES, t // SUBLANES, d), F32)
    else:
        out_spec, out_shape = tok, jax.ShapeDtypeStruct((b, t, d), F32)
    out = pl.pallas_call(
        functools.partial(_ffn_kernel, final_norm=final_norm),
        grid=(b, t // TM),
        in_specs=[
            tok,
            pl.BlockSpec((1, N_MOD, d), lambda bb, i: (jnp.where(i < ctx_tiles, b, bb), 0, 0)),
            _layer_spec(w1.shape, layer), _layer_spec(w3.shape, layer), _layer_spec(w2.shape, layer),
            _const_spec(final_g.shape),
        ],
        out_specs=out_spec,
        out_shape=out_shape,
        compiler_params=_params(2),
        name="ffn",
    )(x_all, mod, w1, w3, w2, final_g)
    return out.reshape(b, t, d)


def _piece_order(n):
    r = np.arange(n, dtype=np.int64)
    return (r % SUBLANES) * (n // SUBLANES) + r // SUBLANES


def _rope_tables(n_ctx, n_lat):
    p = _piece_order(n_lat)
    rows = (p // GRID_W).astype(np.float32)
    cols = (p % GRID_W).astype(np.float32)
    n_freq = HEAD_DIM // 4
    inv = np.float32(ROPE_BASE) ** (-np.arange(n_freq, dtype=np.float32) / np.float32(n_freq))
    ang = np.concatenate([rows[:, None] * inv, cols[:, None] * inv], axis=-1)
    cos, sin = np.cos(ang), np.sin(ang)
    reps = LANES // HEAD_DIM
    cos_l = np.tile(np.concatenate([cos, cos], axis=-1), (1, reps))
    sin_l = np.tile(np.concatenate([-sin, sin], axis=-1), (1, reps))
    cos_t = np.concatenate([np.ones((n_ctx, LANES), np.float32), cos_l], axis=0)
    sin_t = np.concatenate([np.zeros((n_ctx, LANES), np.float32), sin_l], axis=0)
    return jnp.asarray(cos_t, F32), jnp.asarray(sin_t, F32)


def _dft_cos_sin(n, k, t):
    ang = ((k[:, None] * t[None, :]) % n) * (2.0 * math.pi / n)
    return np.cos(ang), np.sin(ang)


def _dft_parity_tables(n):
    steps, half = n // SUBLANES, SUBLANES // 2
    r_in = np.arange(n // 2, dtype=np.int64)
    step_in, piece_in = r_in // half, r_in % half
    t = piece_in * steps + step_in
    planes = []
    for par in range(2):
        step_out = np.arange(par, steps, 2, dtype=np.int64)
        k = (np.arange(SUBLANES, dtype=np.int64)[None, :] * steps + step_out[:, None]).reshape(-1)
        c, s = _dft_cos_sin(n, k, t)
        sign = np.where((step_in % 2 == 1) & (par == 1), -1.0, 1.0)
        planes.append(np.concatenate([c * sign, -s * sign], axis=1))
    return np.stack(planes)


def _dft_position_matrix(n):
    return jnp.asarray(_dft_parity_tables(n), F32).astype(BF16)


def kernel(x, c, ctx, c_ctx, ada_w, ada_b, w_in, rnn_conv_w, rnn_conv_b, rnn_wr, rnn_br, rnn_wi, rnn_bi,
           rnn_lambda, attn_lambda, w_branch, w_gate, b_gate, w_out, ffn_w1, ffn_w3, ffn_w2, final_g):
    b, n_lat, d = x.shape
    n_ctx = ctx.shape[1]
    depth = ada_w.shape[0]
    ctx_tiles = n_ctx // TM
    assert n_ctx == TM and n_lat % TM == 0

    cond = jnp.zeros((2 * SUBLANES, d), F32).at[:b].set(c).at[b].set(c_ctx)
    mod_all = _adaln(cond, ada_w, ada_b.reshape(depth, 1, N_MOD * d))

    cos_t, sin_t = _rope_tables(n_ctx, n_lat)
    ch = np.arange(FOURIER_GD, dtype=np.int64)
    cc, sc = _dft_cos_sin(FOURIER_GD, ch, ch)
    wf = jnp.asarray(np.concatenate([cc, sc], axis=1), F32).astype(BF16)
    a_lat = _dft_position_matrix(n_lat)
    a_ctx = _dft_position_matrix(n_ctx)
    fg = final_g.reshape(1, d)

    x_all = (ctx, x)
    lam_inits = tuple(0.8 - 0.6 * math.exp(-0.3 * l) for l in range(depth))
    lam_rows = _attn_lambda(attn_lambda, lam_inits)
    for l in range(depth):
        last = l == depth - 1
        mod = mod_all[l].reshape(2 * SUBLANES, N_MOD, d)
        outs = _inproj(x_all, mod, w_in, l, wf, cos_t, sin_t)
        if l == 0:
            x_all, outs = outs[0], outs[1:]
        h, xr, gr, xce, xse, xco, xso, qt0, qt1, k, vt = outs
        ya = _rglru(xr, gr, rnn_conv_w[l], rnn_conv_b[l].reshape(1, d), rnn_wr[l].astype(BF16), rnn_br[l],
                    rnn_wi[l].astype(BF16), rnn_bi[l], rnn_lambda[l], n_ctx)
        yb = _fourier(a_lat, a_ctx, xce, xse, xco, xso, n_ctx)
        yc = _attention(lam_rows[l], qt0, qt1, k, vt, n_ctx, lam_inits[l])
        tile0 = ctx_tiles if last else 0
        x_mid = _merge(x_all, mod, h, ya, yb, yc, w_branch, w_gate, b_gate[l].reshape(1, -1), w_out, l, tile0)
        x_all = _ffn(x_mid, mod, ffn_w1, ffn_w3, ffn_w2, l, fg, ctx_tiles - tile0, last)
    return x_all
```
